```python
import jax, jax.numpy as jnp
from jax import lax
import numpy as np

D_MODEL = 1024
BATCH = 4
SEQ = 4096
DEPTH = 1

MEM_LEN = 256
EPS = 1e-6

SB_HEADS = 16
SB_HEAD_DIM = 64
SB_WIDTH = SB_HEADS * SB_HEAD_DIM
SB_BLOCK = 128

SSD_EXPAND = 2
SSD_INNER = SSD_EXPAND * D_MODEL
SSD_HEAD_DIM = 64
SSD_HEADS = SSD_INNER // SSD_HEAD_DIM
SSD_GROUPS = 4
SSD_HEADS_PER_GROUP = SSD_HEADS // SSD_GROUPS
SSD_STATE = 128
SSD_CONV = 4
SSD_CHUNK = 128
SSD_CONV_DIM = SSD_INNER + 2 * SSD_GROUPS * SSD_STATE

MEM_HEADS = 4
MEM_HEAD_DIM = 256
MEM_WIDTH = MEM_HEADS * MEM_HEAD_DIM

N_BRANCHES = 3
D_FF = 4 * D_MODEL

IN_SIZES = (3 * SB_WIDTH, SSD_INNER, SSD_CONV_DIM, SSD_HEADS, MEM_WIDTH, N_BRANCHES * D_MODEL)
D_IN_PROJ = int(sum(IN_SIZES))
SPLIT_POINTS = tuple(int(v) for v in np.cumsum(IN_SIZES)[:-1])

kernel_name = "hybrid_gated_stickbreak_ssd_memxattn_block"


def rms_norm(x, gain):
    xf = x.astype(jnp.float32)
    y = xf * lax.rsqrt(jnp.mean(xf * xf, axis=-1, keepdims=True) + EPS)
    return (y * gain.astype(jnp.float32)).astype(x.dtype)


def stick_breaking_attention(q, k, v):
    b, h, s, dh = q.shape
    nb = s // SB_BLOCK
    scale = dh ** -0.5
    qb = q.reshape(b, h, nb, SB_BLOCK, dh).transpose(2, 0, 1, 3, 4)
    key_pos = jnp.arange(s)

    def block(args):
        q_blk, blk_idx = args
        q_pos = blk_idx * SB_BLOCK + jnp.arange(SB_BLOCK)
        z = jnp.einsum("bhqd,bhkd->bhqk", q_blk, k).astype(jnp.float32) * scale
        causal = key_pos[None, :] < q_pos[:, None]
        log_beta = jax.nn.log_sigmoid(z)
        log_keep = jnp.where(causal, jax.nn.log_sigmoid(-z), 0.0)
        suffix = lax.cumsum(log_keep, axis=3, reverse=True) - log_keep
        w = jnp.where(causal, jnp.exp(log_beta + suffix), 0.0)
        return jnp.einsum("bhqk,bhkd->bhqd", w.astype(v.dtype), v)

    out = lax.map(block, (qb, jnp.arange(nb)))
    return out.transpose(1, 2, 0, 3, 4).reshape(b, h, s, dh)


def causal_depthwise_conv(x, w, bias):
    kw, c = w.shape
    y = lax.conv_general_dilated(
        x, w[:, None, :].astype(x.dtype), window_strides=(1,), padding=((kw - 1, 0),),
        dimension_numbers=("NWC", "WIO", "NWC"), feature_group_count=c)
    return y + bias.astype(x.dtype)


def ssd_chunked(xh, dt, a, bmat, cmat):
    b, s, g, r, p = xh.shape
    n = bmat.shape[-1]
    l = SSD_CHUNK
    c = s // l
    x = (xh * dt[..., None]).reshape(b, c, l, g, r, p)
    da = (dt.astype(jnp.float32) * a.astype(jnp.float32)).reshape(b, c, l, g, r)
    bm = bmat.reshape(b, c, l, g, n)
    cm = cmat.reshape(b, c, l, g, n)
    a_cs = jnp.cumsum(da, axis=2).transpose(0, 1, 3, 4, 2)

    tri = jnp.tril(jnp.ones((l, l), dtype=bool))
    seg = a_cs[..., :, None] - a_cs[..., None, :]
    decay = jnp.exp(jnp.where(tri, seg, -jnp.inf))
    cb = jnp.einsum("bclgn,bcsgn->bcgls", cm, bm).astype(jnp.float32)
    w_intra = cb[:, :, :, None] * decay
    y_diag = jnp.einsum("bcgrls,bcsgrp->bclgrp", w_intra, x.astype(jnp.float32))

    decay_to_end = jnp.exp(a_cs[..., -1:] - a_cs)
    states = jnp.einsum("bcsgn,bcgrs,bcsgrp->bcgrpn", bm.astype(jnp.float32), decay_to_end,
                        x.astype(jnp.float32))
    chunk_decay = jnp.exp(a_cs[..., -1])

    def step(hstate, inp):
        st, dec = inp
        return hstate * dec[..., None, None] + st, hstate

    h0 = jnp.zeros((b, g, r, p, n), jnp.float32)
    _, prev = lax.scan(step, h0, (states.transpose(1, 0, 2, 3, 4, 5), chunk_decay.transpose(1, 0, 2, 3)))
    prev = prev.transpose(1, 0, 2, 3, 4, 5)
    y_off = jnp.einsum("bclgn,bcgrpn,bcgrl->bclgrp", cm.astype(jnp.float32), prev, jnp.exp(a_cs))
    return (y_diag + y_off).reshape(b, s, g, r, p).astype(xh.dtype)


def ssd_branch(z, xbc_raw, dt_raw, conv_w, conv_b, dt_bias, a_log, d_skip, ssd_norm):
    b, s, _ = z.shape
    xbc = jax.nn.silu(causal_depthwise_conv(xbc_raw, conv_w, conv_b))
    xs, bmat, cmat = jnp.split(xbc, (SSD_INNER, SSD_INNER + SSD_GROUPS * SSD_STATE), axis=-1)
    xh = xs.reshape(b, s, SSD_GROUPS, SSD_HEADS_PER_GROUP, SSD_HEAD_DIM)
    bmat = bmat.reshape(b, s, SSD_GROUPS, SSD_STATE)
    cmat = cmat.reshape(b, s, SSD_GROUPS, SSD_STATE)
    dt = jax.nn.softplus((dt_raw + dt_bias).astype(jnp.float32)).astype(xs.dtype)
    dt = dt.reshape(b, s, SSD_GROUPS, SSD_HEADS_PER_GROUP)
    a = -jnp.exp(a_log.astype(jnp.float32)).reshape(SSD_GROUPS, SSD_HEADS_PER_GROUP)
    d = d_skip.reshape(SSD_GROUPS, SSD_HEADS_PER_GROUP)
    y = ssd_chunked(xh, dt, a, bmat, cmat) + d[..., None] * xh
    y = y.reshape(b, s, SSD_INNER) * jax.nn.silu(z)
    yg = y.reshape(b, s, SSD_GROUPS, SSD_INNER // SSD_GROUPS).astype(jnp.float32)
    yg = yg * lax.rsqrt(jnp.mean(yg * yg, axis=-1, keepdims=True) + EPS)
    return (yg.reshape(b, s, SSD_INNER) * ssd_norm.astype(jnp.float32)).astype(z.dtype)


def mem_cross_attention(q, mem, norm_mem, w_mem_kv):
    b, s, _ = q.shape
    m = mem.shape[1]
    kv = rms_norm(mem, norm_mem) @ w_mem_kv
    k, v = jnp.split(kv, 2, axis=-1)
    qh = q.reshape(b, s, MEM_HEADS, MEM_HEAD_DIM)
    kh = k.reshape(b, m, MEM_HEADS, MEM_HEAD_DIM)
    vh = v.reshape(b, m, MEM_HEADS, MEM_HEAD_DIM)
    scores = jnp.einsum("bshd,bmhd->bhsm", qh, kh).astype(jnp.float32) * (MEM_HEAD_DIM ** -0.5)
    probs = jax.nn.softmax(scores, axis=-1).astype(v.dtype)
    return jnp.einsum("bhsm,bmhd->bshd", probs, vh).reshape(b, s, MEM_WIDTH)


def setup_inputs(seed: int = 0) -> dict:
    key = jax.random.key(seed)
    ks = jax.random.split(key, 24)
    nrm = jax.random.normal
    f32 = jnp.float32

    def gain(k, n):
        return 1.0 + 0.05 * nrm(k, (DEPTH, n), f32)

    dt0 = jnp.exp(jax.random.uniform(ks[6], (DEPTH, SSD_HEADS), f32, minval=np.log(1e-3), maxval=np.log(1e-1)))
    dt_bias = dt0 + jnp.log(-jnp.expm1(-dt0))
    return {
        "x": nrm(ks[0], (BATCH, SEQ, D_MODEL), f32),
        "mem": nrm(ks[1], (BATCH, MEM_LEN, D_MODEL), f32),
        "norm_mix_pre": gain(ks[2], D_MODEL),
        "w_in": nrm(ks[3], (DEPTH, D_MODEL, D_IN_PROJ), f32) * D_MODEL ** -0.5,
        "conv_w": nrm(ks[4], (DEPTH, SSD_CONV, SSD_CONV_DIM), f32) * SSD_CONV ** -0.5,
        "conv_b": 0.02 * nrm(ks[5], (DEPTH, SSD_CONV_DIM), f32),
        "dt_bias": dt_bias,
        "a_log": jnp.log(jax.random.uniform(ks[7], (DEPTH, SSD_HEADS), f32, minval=1.0, maxval=16.0)),
        "d_skip": 1.0 + 0.1 * nrm(ks[8], (DEPTH, SSD_HEADS), f32),
        "ssd_norm": gain(ks[9], SSD_INNER),
        "norm_mem": gain(ks[10], D_MODEL),
        "w_mem_kv": nrm(ks[11], (DEPTH, D_MODEL, 2 * MEM_WIDTH), f32) * D_MODEL ** -0.5,
        "w_sb_out": nrm(ks[12], (DEPTH, SB_WIDTH, D_MODEL), f32) * SB_WIDTH ** -0.5,
        "w_ssd_out": nrm(ks[13], (DEPTH, SSD_INNER, D_MODEL), f32) * SSD_INNER ** -0.5,
        "w_mem_out": nrm(ks[14], (DEPTH, MEM_WIDTH, D_MODEL), f32) * MEM_WIDTH ** -0.5,
        "w_o": nrm(ks[15], (DEPTH, D_MODEL, D_MODEL), f32) * D_MODEL ** -0.5,
        "norm_mix_post": gain(ks[16], D_MODEL),
        "norm_mlp_pre": gain(ks[17], D_MODEL),
        "w_up": nrm(ks[18], (DEPTH, D_MODEL, D_FF), f32) * D_MODEL ** -0.5,
        "w_down": nrm(ks[19], (DEPTH, D_FF, D_MODEL), f32) * D_FF ** -0.5,
        "norm_mlp_post": gain(ks[20], D_MODEL),
    }


def reference(x, mem, norm_mix_pre, w_in, conv_w, conv_b, dt_bias, a_log, d_skip, ssd_norm,
              norm_mem, w_mem_kv, w_sb_out, w_ssd_out, w_mem_out, w_o, norm_mix_post,
              norm_mlp_pre, w_up, w_down, norm_mlp_post):
    b, s, _ = x.shape
    h = x
    for layer in range(DEPTH):
        u = rms_norm(h, norm_mix_pre[layer])
        proj = u @ w_in[layer]
        sb_qkv, z, xbc_raw, dt_raw, mem_q, gate_logits = jnp.split(proj, SPLIT_POINTS, axis=-1)

        q, k, v = jnp.split(sb_qkv, 3, axis=-1)
        to_heads = lambda t: t.reshape(b, s, SB_HEADS, SB_HEAD_DIM).transpose(0, 2, 1, 3)
        y_sb = stick_breaking_attention(to_heads(q), to_heads(k), to_heads(v))
        y_sb = y_sb.transpose(0, 2, 1, 3).reshape(b, s, SB_WIDTH)

        y_ssd = ssd_branch(z, xbc_raw, dt_raw, conv_w[layer], conv_b[layer], dt_bias[layer],
                           a_log[layer], d_skip[layer], ssd_norm[layer])

        y_mem = mem_cross_attention(mem_q, mem, norm_mem[layer], w_mem_kv[layer])

        gates = jax.nn.sigmoid(gate_logits.astype(jnp.float32)).astype(h.dtype)
        gates = gates.reshape(b, s, N_BRANCHES, D_MODEL)
        merged = (gates[:, :, 0] * (y_sb @ w_sb_out[layer])
                  + gates[:, :, 1] * (y_ssd @ w_ssd_out[layer])
                  + gates[:, :, 2] * (y_mem @ w_mem_out[layer]))
        mix = merged @ w_o[layer]
        h = h + rms_norm(mix, norm_mix_post[layer])

        u = rms_norm(h, norm_mlp_pre[layer])
        ff = jnp.square(jax.nn.relu(u @ w_up[layer])) @ w_down[layer]
        h = h + rms_norm(ff, norm_mlp_post[layer])
    return h
```

```python
import functools

import jax
import jax.numpy as jnp
import numpy as np
from jax import lax
from jax.experimental import pallas as pl
from jax.experimental.pallas import tpu as pltpu

F32 = jnp.float32
BF16 = jnp.bfloat16

EPS = 1e-6
LANES = 128
SB_HEADS = 16
SB_HEAD_DIM = 64
SB_BLOCK = 128
SSD_HEADS = 32
SSD_HEAD_DIM = 64
SSD_GROUPS = 4
SSD_STATE = 128
SSD_CHUNK = 128
SSD_CONV = 4
MEM_HEADS = 4
MEM_HEAD_DIM = 256
VMEM_LIMIT = 56 * 1024 * 1024

SB_EXP_UNDERFLOW = 110.0


def _dot(a, b):
    return jnp.dot(a, b, preferred_element_type=F32)


def _dot_nt(a, b):
    return lax.dot_general(a, b, (((1,), (1,)), ((), ())), preferred_element_type=F32)


def _split_bf16(x, parts):
    out = []
    rem = x
    for _ in range(parts):
        p = rem.astype(BF16)
        out.append(p)
        rem = rem - p.astype(F32)
    return out


def _dot_split_lhs(x, m, parts):
    acc = None
    for p in _split_bf16(x, parts):
        t = _dot(p, m)
        acc = t if acc is None else acc + t
    return acc


def _dot_split_rhs(m, x, parts):
    acc = None
    for p in _split_bf16(x, parts):
        t = _dot(m, p)
        acc = t if acc is None else acc + t
    return acc


def _rms_rows(x, gain):
    ms = jnp.mean(x * x, axis=-1, keepdims=True)
    return x * lax.rsqrt(ms + EPS) * gain


def _softplus(x):
    return jnp.maximum(x, 0.0) + jnp.log1p(jnp.exp(-jnp.abs(x)))


def _sigmoid(x):
    return 1.0 / (1.0 + jnp.exp(-x))


def _norm_proj_kernel(x_ref, g_ref, w_ref, o_ref, u_ref):
    @pl.when(pl.program_id(1) == 0)
    def _():
        u_ref[...] = _rms_rows(x_ref[...], g_ref[...]).astype(BF16)

    o_ref[...] = _dot(u_ref[...], w_ref[...]).astype(o_ref.dtype)


def _norm_proj_dt_kernel(x_ref, g_ref, w_ref, wdt_ref, wdtt_ref, o_ref, dt_ref, dtt_ref, u_ref):
    @pl.when(pl.program_id(1) == 0)
    def _():
        u = _rms_rows(x_ref[...], g_ref[...]).astype(BF16)
        u_ref[...] = u
        dt_ref[...] = _dot(u, wdt_ref[...])
        dtt_ref[...] = _dot_nt(wdtt_ref[...], u)

    o_ref[...] = _dot(u_ref[...], w_ref[...]).astype(o_ref.dtype)


def _norm_proj(x, gain, w, out_dtype, tm, tn, w_dt=None, w_dt_t=None):
    m, d = x.shape
    n = w.shape[1]
    grid = (m // tm, n // tn)
    x_spec = pl.BlockSpec((tm, d), lambda i, j: (i, 0))
    g_spec = pl.BlockSpec((1, d), lambda i, j: (0, 0))
    w_spec = pl.BlockSpec((d, tn), lambda i, j: (0, j))
    o_spec = pl.BlockSpec((tm, tn), lambda i, j: (i, j))
    params = pltpu.CompilerParams(dimension_semantics=("arbitrary", "arbitrary"),
                                  vmem_limit_bytes=VMEM_LIMIT)
    scratch = [pltpu.VMEM((tm, d), BF16)]
    if w_dt is None:
        return pl.pallas_call(
            _norm_proj_kernel, grid=grid,
            in_specs=[x_spec, g_spec, w_spec], out_specs=o_spec,
            out_shape=jax.ShapeDtypeStruct((m, n), out_dtype),
            scratch_shapes=scratch, compiler_params=params, name="norm_proj",
        )(x, gain, w)
    nh = w_dt_t.shape[0]
    return pl.pallas_call(
        _norm_proj_dt_kernel, grid=grid,
        in_specs=[x_spec, g_spec, w_spec,
                  pl.BlockSpec((d, LANES), lambda i, j: (0, 0)),
                  pl.BlockSpec((nh, d), lambda i, j: (0, 0))],
        out_specs=[o_spec,
                   pl.BlockSpec((tm, LANES), lambda i, j: (i, 0)),
                   pl.BlockSpec((nh, tm), lambda i, j: (0, i))],
        out_shape=[jax.ShapeDtypeStruct((m, n), out_dtype),
                   jax.ShapeDtypeStruct((m, LANES), F32),
                   jax.ShapeDtypeStruct((nh, m), F32)],
        scratch_shapes=scratch, compiler_params=params, name="norm_proj_dt",
    )(x, gain, w, w_dt, w_dt_t)


def _sb_kernel(q_ref, k_ref, v_ref, m_ref, o_ref, acc_ref, c_ref, *, nsub):
    qi = pl.program_id(2)
    lane = lax.broadcasted_iota(jnp.int32, (SB_BLOCK, LANES), 1)
    row = lax.broadcasted_iota(jnp.int32, (SB_BLOCK, LANES), 0)
    head0 = lane < SB_HEAD_DIM
    causal = lane < row
    scale = SB_HEAD_DIM ** -0.5

    def head_block(qh, kb, vb, h, diag):
        z = _dot_nt(qh, kb) * scale
        sp = jnp.log1p(jnp.exp(-jnp.abs(z)))
        log_beta = jnp.minimum(z, 0.0) - sp
        log_keep = jnp.minimum(-z, 0.0) - sp
        if diag:
            log_keep = jnp.where(causal, log_keep, 0.0)
        res = _dot_split_lhs(log_keep, m_ref[...], 2)
        c = c_ref[h]
        w = jnp.exp(log_beta + res[:, :SB_BLOCK] + c)
        if diag:
            w = jnp.where(causal, w, 0.0)
        acc_ref[h] += _dot(w.astype(BF16), vb)
        c_new = c + res[:, SB_BLOCK:]
        c_ref[h] = c_new
        return c_new

    def sub(t, carry):
        qblk = qi * nsub + t
        q = q_ref[0, pl.ds(pl.multiple_of(t * SB_BLOCK, SB_BLOCK), SB_BLOCK), :].astype(F32)
        q0 = jnp.where(head0, q, 0.0).astype(BF16)
        q1 = jnp.where(head0, 0.0, q).astype(BF16)
        acc_ref[...] = jnp.zeros_like(acc_ref)
        c_ref[...] = jnp.zeros_like(c_ref)

        def do_block(j, diag):
            off = pl.multiple_of(j * SB_BLOCK, SB_BLOCK)
            kb = k_ref[0, pl.ds(off, SB_BLOCK), :]
            vb = v_ref[0, pl.ds(off, SB_BLOCK), :]
            c0 = head_block(q0, kb, vb, 0, diag)
            c1 = head_block(q1, kb, vb, 1, diag)
            return jnp.max(jnp.maximum(c0, c1))

        cmax = do_block(qblk, True)

        def cond(state):
            j, cm = state
            return jnp.logical_and(j >= 0, cm > -SB_EXP_UNDERFLOW)

        def body(state):
            j, _ = state
            return j - 1, do_block(j, False)

        lax.while_loop(cond, body, (qblk - 1, cmax))
        out = jnp.where(head0, acc_ref[0], acc_ref[1])
        o_ref[0, pl.ds(pl.multiple_of(t * SB_BLOCK, SB_BLOCK), SB_BLOCK), :] = out.astype(o_ref.dtype)
        return carry

    lax.fori_loop(0, nsub, sub, 0)


def _sb_attention(qkv, b, s, tq):
    pairs = SB_HEADS * SB_HEAD_DIM // LANES
    nsub = tq // SB_BLOCK
    idx = np.arange(SB_BLOCK)
    suffix = (idx[:, None] > idx[None, :]).astype(np.float32)
    m = jnp.asarray(np.concatenate([suffix, np.ones((SB_BLOCK, SB_BLOCK), np.float32)], axis=1), BF16)
    return pl.pallas_call(
        functools.partial(_sb_kernel, nsub=nsub),
        grid=(b, pairs, s // tq),
        in_specs=[
            pl.BlockSpec((1, tq, LANES), lambda bi, hp, qi: (bi, qi, hp)),
            pl.BlockSpec((1, s, LANES), lambda bi, hp, qi: (bi, 0, pairs + hp)),
            pl.BlockSpec((1, s, LANES), lambda bi, hp, qi: (bi, 0, 2 * pairs + hp)),
            pl.BlockSpec((SB_BLOCK, 2 * SB_BLOCK), lambda bi, hp, qi: (0, 0)),
        ],
        out_specs=pl.BlockSpec((1, tq, LANES), lambda bi, hp, qi: (bi, qi, hp)),
        out_shape=jax.ShapeDtypeStruct((b, s, SB_HEADS * SB_HEAD_DIM), BF16),
        scratch_shapes=[pltpu.VMEM((2, SB_BLOCK, LANES), F32), pltpu.VMEM((2, SB_BLOCK, LANES), F32)],
        compiler_params=pltpu.CompilerParams(
            dimension_semantics=("arbitrary", "arbitrary", "arbitrary"), vmem_limit_bytes=VMEM_LIMIT),
        name="sb_attention",
    )(qkv, qkv, qkv, m)


def _ssd_kernel(z_ref, xs_ref, bc_ref, dt_ref, dtt_ref,
                cwx_ref, cbx_ref, cwb_ref, cbb_ref,
                dtb_ref, dtbt_ref, alog_ref, alogt_ref, dexp_ref, gn_ref,
                tri_ref, trit_ref, e_ref,
                o_ref, tailx_ref, tailb_ref, state_ref, ydiag_ref):
    ck = SSD_CHUNK
    inner = SSD_HEADS * SSD_HEAD_DIM
    gw = inner // SSD_GROUPS
    gs = SSD_GROUPS * SSD_STATE

    @pl.when(pl.program_id(1) == 0)
    def _():
        tailx_ref[...] = jnp.zeros_like(tailx_ref)
        tailb_ref[...] = jnp.zeros_like(tailb_ref)
        state_ref[...] = jnp.zeros_like(state_ref)

    def conv_silu(x, tail_ref, w_ref, b_ref):
        tail = tail_ref[...]
        row8 = lax.broadcasted_iota(jnp.int32, tail.shape, 0)
        acc = x * w_ref[SSD_CONV - 1:SSD_CONV, :] + b_ref[...]
        for k in range(1, SSD_CONV):
            xr = pltpu.roll(x, k, 0)
            first = jnp.where(row8 < k, pltpu.roll(tail, k, 0), xr[0:8])
            xk = jnp.concatenate([first, xr[8:]], axis=0)
            acc = acc + xk * w_ref[SSD_CONV - 1 - k:SSD_CONV - k, :]
        tail_ref[...] = x[ck - 8:ck]
        return acc * _sigmoid(acc)

    xs = conv_silu(xs_ref[0], tailx_ref, cwx_ref, cbx_ref)
    bcm = conv_silu(bc_ref[0], tailb_ref, cwb_ref, cbb_ref)

    dt = _softplus(dt_ref[...] + dtb_ref[...])
    dtt = _softplus(dtt_ref[...] + dtbt_ref[...])
    da = dt * (-jnp.exp(alog_ref[...]))
    dat = dtt * (-jnp.exp(alogt_ref[...]))
    a_cs = _dot_split_rhs(tri_ref[...], da, 3)
    a_cst = _dot_split_lhs(dat, trit_ref[...], 3)
    a_last = a_cs[ck - 1:ck, :]

    e = e_ref[...]
    dt_x = _dot_split_lhs(dt, e, 2)
    ea_x = _dot_split_lhs(jnp.exp(a_cs), e, 2)
    dte_x = _dot_split_lhs(jnp.exp(a_last - a_cs), e, 2)

    x_dt = xs * dt_x
    x_dt16 = x_dt.astype(BF16)
    x_end16 = (x_dt * dte_x).astype(BF16)

    lane = lax.broadcasted_iota(jnp.int32, (ck, LANES), 1)
    row = lax.broadcasted_iota(jnp.int32, (ck, LANES), 0)
    head0 = lane < SSD_HEAD_DIM
    tri_mask = lane <= row

    heads_per_group = SSD_HEADS // SSD_GROUPS
    for g in range(SSD_GROUPS):
        bg = bcm[:, g * SSD_STATE:(g + 1) * SSD_STATE]
        cg16 = bcm[:, gs + g * SSD_STATE:gs + (g + 1) * SSD_STATE].astype(BF16)
        cb = _dot_nt(cg16, bg.astype(BF16))
        for rp in range(heads_per_group // 2):
            pair = g * (heads_per_group // 2) + rp
            xp = x_dt16[:, pair * LANES:(pair + 1) * LANES]
            ys = []
            for hh in range(2):
                r = 2 * pair + hh
                seg = a_cs[:, r:r + 1] - a_cst[r:r + 1, :]
                lmat = cb * jnp.exp(jnp.where(tri_mask, seg, -jnp.inf))
                ys.append(_dot(lmat.astype(BF16), xp))
            ydiag_ref[:, pair * LANES:(pair + 1) * LANES] = jnp.where(head0, ys[0], ys[1])

        cols = slice(g * gw, (g + 1) * gw)
        st = state_ref[:, cols]
        y_off = _dot(cg16, st.astype(BF16)) * ea_x[:, cols]
        contrib = _dot(bg.T.astype(BF16), x_end16[:, cols])
        state_ref[:, cols] = st * ea_x[ck - 1:ck, cols] + contrib

        y = ydiag_ref[:, cols] + y_off + dexp_ref[:, cols] * xs[:, cols]
        zg = z_ref[0, :, cols]
        y = y * (zg * _sigmoid(zg))
        ms = jnp.mean(y * y, axis=-1, keepdims=True)
        o_ref[0, :, cols] = (y * lax.rsqrt(ms + EPS) * gn_ref[:, cols]).astype(o_ref.dtype)


def _ssd_branch(proj, dt, dtt, b, s, conv_w, conv_b, dt_bias, a_log, d_skip, ssd_norm):
    inner = SSD_HEADS * SSD_HEAD_DIM
    gs = SSD_GROUPS * SSD_STATE
    nc = s // SSD_CHUNK
    idx = np.arange(SSD_CHUNK)
    tri = jnp.asarray((idx[None, :] <= idx[:, None]).astype(np.float32), BF16)
    trit = jnp.asarray((idx[:, None] <= idx[None, :]).astype(np.float32), BF16)
    expand = np.zeros((LANES, inner), np.float32)
    expand[np.arange(inner) // SSD_HEAD_DIM, np.arange(inner)] = 1.0
    expand = jnp.asarray(expand, BF16)

    pad = LANES - SSD_HEADS
    row = lambda v: v.reshape(1, -1)
    dtb = jnp.pad(row(dt_bias), ((0, 0), (0, pad)))
    alog = jnp.pad(row(a_log), ((0, 0), (0, pad)))
    dexp = row(jnp.repeat(d_skip, SSD_HEAD_DIM))

    const = lambda shape: pl.BlockSpec(shape, lambda bi, ci: (0,) * len(shape))
    return pl.pallas_call(
        _ssd_kernel, grid=(b, nc),
        in_specs=[
            pl.BlockSpec((1, SSD_CHUNK, inner), lambda bi, ci: (bi, ci, 0)),
            pl.BlockSpec((1, SSD_CHUNK, inner), lambda bi, ci: (bi, ci, 1)),
            pl.BlockSpec((1, SSD_CHUNK, 2 * gs), lambda bi, ci: (bi, ci, 2 * inner // (2 * gs))),
            pl.BlockSpec((SSD_CHUNK, LANES), lambda bi, ci: (bi * nc + ci, 0)),
            pl.BlockSpec((SSD_HEADS, SSD_CHUNK), lambda bi, ci: (0, bi * nc + ci)),
            const((SSD_CONV, inner)), const((1, inner)),
            const((SSD_CONV, 2 * gs)), const((1, 2 * gs)),
            const((1, LANES)), const((SSD_HEADS, 1)), const((1, LANES)), const((SSD_HEADS, 1)),
            const((1, inner)), const((1, inner)),
            const((SSD_CHUNK, SSD_CHUNK)), const((SSD_CHUNK, SSD_CHUNK)), const((LANES, inner)),
        ],
        out_specs=pl.BlockSpec((1, SSD_CHUNK, inner), lambda bi, ci: (bi, ci, 0)),
        out_shape=jax.ShapeDtypeStruct((b, s, inner), BF16),
        scratch_shapes=[pltpu.VMEM((8, inner), F32), pltpu.VMEM((8, 2 * gs), F32),
                        pltpu.VMEM((SSD_STATE, inner), F32), pltpu.VMEM((SSD_CHUNK, inner), F32)],
        compiler_params=pltpu.CompilerParams(
            dimension_semantics=("arbitrary", "arbitrary"), vmem_limit_bytes=VMEM_LIMIT),
        name="ssd_branch",
    )(proj, proj, proj, dt, dtt,
      conv_w[:, :inner], row(conv_b[:inner]), conv_w[:, inner:], row(conv_b[inner:]),
      dtb, dt_bias.reshape(-1, 1), alog, a_log.reshape(-1, 1), dexp, row(ssd_norm),
      tri, trit, expand)


def _mem_attn_kernel(q_ref, kv_ref, o_ref):
    width = MEM_HEADS * MEM_HEAD_DIM
    scale = MEM_HEAD_DIM ** -0.5
    for h in range(MEM_HEADS):
        cols = slice(h * MEM_HEAD_DIM, (h + 1) * MEM_HEAD_DIM)
        q = q_ref[0, :, cols]
        k = kv_ref[0, :, cols]
        v = kv_ref[0, :, width + h * MEM_HEAD_DIM:width + (h + 1) * MEM_HEAD_DIM]
        sc = _dot_nt(q, k) * scale
        p = jnp.exp(sc - jnp.max(sc, axis=-1, keepdims=True))
        probs = p / jnp.sum(p, axis=-1, keepdims=True)
        o_ref[0, :, cols] = _dot(probs.astype(BF16), v).astype(o_ref.dtype)


def _mem_attention(qkv, kv, b, s, tq):
    width = MEM_HEADS * MEM_HEAD_DIM
    mlen = kv.shape[1]
    return pl.pallas_call(
        _mem_attn_kernel, grid=(b, s // tq),
        in_specs=[pl.BlockSpec((1, tq, width), lambda bi, qi: (bi, qi, 3)),
                  pl.BlockSpec((1, mlen, 2 * width), lambda bi, qi: (bi, 0, 0))],
        out_specs=pl.BlockSpec((1, tq, width), lambda bi, qi: (bi, qi, 0)),
        out_shape=jax.ShapeDtypeStruct((b, s, width), BF16),
        compiler_params=pltpu.CompilerParams(
            dimension_semantics=("arbitrary", "arbitrary"), vmem_limit_bytes=VMEM_LIMIT),
        name="mem_attention",
    )(qkv, kv)


def _merge_kernel(x_ref, ysb_ref, yssd_ref, ymem_ref, g0_ref, g1_ref, g2_ref,
                  wsb_ref, wssd_ref, wmem_ref, wo_ref, gain_ref, o_ref):
    merged = (_sigmoid(g0_ref[...]) * _dot(ysb_ref[...], wsb_ref[...])
              + _sigmoid(g1_ref[...]) * _dot(yssd_ref[...], wssd_ref[...])
              + _sigmoid(g2_ref[...]) * _dot(ymem_ref[...], wmem_ref[...]))
    mix = _dot(merged.astype(BF16), wo_ref[...])
    o_ref[...] = x_ref[...] + _rms_rows(mix, gain_ref[...])


def _merge(x, y_sb, y_ssd, y_mem, proj, w_sb, w_ssd, w_mem, w_o, gain, tm):
    m, d = x.shape
    inner = y_ssd.shape[1]
    gate0 = proj.shape[1] // d - 3
    tile = lambda w: pl.BlockSpec((tm, w), lambda i: (i, 0))
    gate = lambda k: pl.BlockSpec((tm, d), lambda i: (i, gate0 + k))
    full = lambda a: pl.BlockSpec(a.shape, lambda i: (0, 0))
    return pl.pallas_call(
        _merge_kernel, grid=(m // tm,),
        in_specs=[tile(d), tile(d), tile(inner), tile(d), gate(0), gate(1), gate(2),
                  full(w_sb), full(w_ssd), full(w_mem), full(w_o), full(gain)],
        out_specs=tile(d),
        out_shape=jax.ShapeDtypeStruct((m, d), F32),
        compiler_params=pltpu.CompilerParams(
            dimension_semantics=("arbitrary",), vmem_limit_bytes=VMEM_LIMIT),
        name="merge",
    )(x, y_sb, y_ssd, y_mem, proj, proj, proj, w_sb, w_ssd, w_mem, w_o, gain)


def _mlp_kernel(h_ref, gpre_ref, wup_ref, wdown_ref, gpost_ref, o_ref, *, chunk):
    h = h_ref[...]
    u = _rms_rows(h, gpre_ref[...]).astype(BF16)
    d_ff = wup_ref.shape[1]
    ff = None
    for c in range(d_ff // chunk):
        hid = _dot(u, wup_ref[:, c * chunk:(c + 1) * chunk])
        act = jnp.square(jnp.maximum(hid, 0.0)).astype(BF16)
        t = _dot(act, wdown_ref[c * chunk:(c + 1) * chunk, :])
        ff = t if ff is None else ff + t
    o_ref[...] = h + _rms_rows(ff, gpost_ref[...])


def _mlp(h, g_pre, w_up, w_down, g_post, tm):
    m, d = h.shape
    full = lambda a: pl.BlockSpec(a.shape, lambda i: (0, 0))
    tile = pl.BlockSpec((tm, d), lambda i: (i, 0))
    return pl.pallas_call(
        functools.partial(_mlp_kernel, chunk=1024), grid=(m // tm,),
        in_specs=[tile, full(g_pre), full(w_up), full(w_down), full(g_post)],
        out_specs=tile,
        out_shape=jax.ShapeDtypeStruct((m, d), F32),
        compiler_params=pltpu.CompilerParams(
            dimension_semantics=("arbitrary",), vmem_limit_bytes=VMEM_LIMIT),
        name="mlp",
    )(h, g_pre, w_up, w_down, g_post)


def _layer(h, mem, norm_mix_pre, w_in, conv_w, conv_b, dt_bias, a_log, d_skip, ssd_norm,
           norm_mem, w_mem_kv, w_sb_out, w_ssd_out, w_mem_out, w_o, norm_mix_post,
           norm_mlp_pre, w_up, w_down, norm_mlp_post):
    b, s, d = h.shape
    m = b * s
    row = lambda v: v.reshape(1, -1)
    x2 = h.reshape(m, d)

    sb_w = 3 * SB_HEADS * SB_HEAD_DIM
    inner = SSD_HEADS * SSD_HEAD_DIM
    conv_dim = inner + 2 * SSD_GROUPS * SSD_STATE
    o_z, o_xbc, o_dt = sb_w, sb_w + inner, sb_w + inner + conv_dim
    o_memq = o_dt + SSD_HEADS
    o_gate = o_memq + MEM_HEADS * MEM_HEAD_DIM

    w_a = jnp.concatenate([w_in[:, :sb_w], w_in[:, o_memq:o_gate]], axis=1).astype(BF16)
    w_b = jnp.concatenate([w_in[:, o_z:o_dt], w_in[:, o_gate:]], axis=1).astype(BF16)
    w_dt = w_in[:, o_dt:o_memq]
    w_dt_pad = jnp.pad(w_dt, ((0, 0), (0, LANES - SSD_HEADS))).astype(BF16)
    w_dt_t = w_dt.T.astype(BF16)

    g_pre = row(norm_mix_pre)
    qkv = _norm_proj(x2, g_pre, w_a, BF16, 1024, 1024).reshape(b, s, -1)
    proj, dt, dtt = _norm_proj(x2, g_pre, w_b, F32, 1024, 1024, w_dt_pad, w_dt_t)

    y_sb = _sb_attention(qkv, b, s, 512)
    y_ssd = _ssd_branch(proj.reshape(b, s, -1), dt, dtt, b, s,
                        conv_w, conv_b, dt_bias, a_log, d_skip, ssd_norm)

    mlen = mem.shape[1]
    kv = _norm_proj(mem.reshape(b * mlen, d), row(norm_mem), w_mem_kv.astype(BF16), BF16, b * mlen, 1024)
    y_mem = _mem_attention(qkv, kv.reshape(b, mlen, -1), b, s, 512)

    h1 = _merge(x2, y_sb.reshape(m, -1), y_ssd.reshape(m, -1), y_mem.reshape(m, -1), proj,
                w_sb_out.astype(BF16), w_ssd_out.astype(BF16), w_mem_out.astype(BF16),
                w_o.astype(BF16), row(norm_mix_post), 256)
    out = _mlp(h1, row(norm_mlp_pre), w_up.astype(BF16), w_down.astype(BF16), row(norm_mlp_post), 256)
    return out.reshape(b, s, d)


def kernel(x, mem, norm_mix_pre, w_in, conv_w, conv_b, dt_bias, a_log, d_skip, ssd_norm, norm_mem, w_mem_kv, w_sb_out, w_ssd_out, w_mem_out, w_o, norm_mix_post, norm_mlp_pre, w_up, w_down, norm_mlp_post):
    h = x
    for layer in range(w_in.shape[0]):
        h = _layer(h, mem, norm_mix_pre[layer], w_in[layer], conv_w[layer], conv_b[layer],
                   dt_bias[layer], a_log[layer], d_skip[layer], ssd_norm[layer], norm_mem[layer],
                   w_mem_kv[layer], w_sb_out[layer], w_ssd_out[layer], w_mem_out[layer], w_o[layer],
                   norm_mix_post[layer], norm_mlp_pre[layer], w_up[layer], w_down[layer],
                   norm_mlp_post[layer])
    return h
```

```python
import functools

import jax
import jax.numpy as jnp
import numpy as np
from jax import lax
from jax.experimental import pallas as pl
from jax.experimental.pallas import tpu as pltpu

F32 = jnp.float32
BF16 = jnp.bfloat16

EPS = 1e-6
LOG2E = 1.4426950408889634
LANES = 128
SB_HEADS = 16
SB_HEAD_DIM = 64
SB_BLOCK = 128
SSD_HEADS = 32
SSD_HEAD_DIM = 64
SSD_GROUPS = 4
SSD_STATE = 128
SSD_CHUNK = 128
SSD_CONV = 4
MEM_HEADS = 4
MEM_HEAD_DIM = 256
VMEM_LIMIT = 56 * 1024 * 1024

SB_EXP_UNDERFLOW = 110.0
SB_NO_KEYS = 1e30


def _dot(a, b):
    return jnp.dot(a, b, preferred_element_type=F32)


def _dot_nt(a, b):
    return lax.dot_general(a, b, (((1,), (1,)), ((), ())), preferred_element_type=F32)


def _split_bf16(x, parts):
    out = []
    rem = x
    for _ in range(parts):
        p = rem.astype(BF16)
        out.append(p)
        rem = rem - p.astype(F32)
    return out


def _dot_split_lhs(x, m, parts):
    acc = None
    for p in _split_bf16(x, parts):
        t = _dot(p, m)
        acc = t if acc is None else acc + t
    return acc


def _dot_split_rhs(m, x, parts):
    acc = None
    for p in _split_bf16(x, parts):
        t = _dot(m, p)
        acc = t if acc is None else acc + t
    return acc


def _rms_rows(x, gain):
    ms = jnp.mean(x * x, axis=-1, keepdims=True)
    return x * lax.rsqrt(ms + EPS) * gain


def _softplus(x):
    return jnp.maximum(x, 0.0) + jnp.log1p(jnp.exp(-jnp.abs(x)))


def _sigmoid(x):
    return 1.0 / (1.0 + jnp.exp(-x))


def _norm_proj_kernel(x_ref, g_ref, w_ref, o_ref, u_ref):
    @pl.when(pl.program_id(1) == 0)
    def _():
        u_ref[...] = _rms_rows(x_ref[...], g_ref[...]).astype(BF16)

    o_ref[...] = _dot(u_ref[...], w_ref[...]).astype(o_ref.dtype)


def _norm_proj(x, gain, w, out_dtype, tm, tn):
    m, d = x.shape
    n = w.shape[1]
    return pl.pallas_call(
        _norm_proj_kernel, grid=(m // tm, n // tn),
        in_specs=[pl.BlockSpec((tm, d), lambda i, j: (i, 0)),
                  pl.BlockSpec((1, d), lambda i, j: (0, 0)),
                  pl.BlockSpec((d, tn), lambda i, j: (0, j))],
        out_specs=pl.BlockSpec((tm, tn), lambda i, j: (i, j)),
        out_shape=jax.ShapeDtypeStruct((m, n), out_dtype),
        scratch_shapes=[pltpu.VMEM((tm, d), BF16)],
        compiler_params=pltpu.CompilerParams(dimension_semantics=("arbitrary", "arbitrary"),
                                             vmem_limit_bytes=VMEM_LIMIT),
        name="norm_proj",
    )(x, gain, w)


def _in_proj_kernel(x_ref, g_ref, w_ref, wdt_ref, wdtt_ref, oa_ref, ob_ref, dt_ref, dtt_ref, u_ref, *, na):
    j = pl.program_id(1)

    @pl.when(j == 0)
    def _():
        u = _rms_rows(x_ref[...], g_ref[...]).astype(BF16)
        u_ref[...] = u
        dt_ref[...] = _dot(u, wdt_ref[...])
        dtt_ref[...] = _dot_nt(wdtt_ref[...], u)

    @pl.when(j < na)
    def _():
        oa_ref[...] = _dot(u_ref[...], w_ref[...]).astype(oa_ref.dtype)

    @pl.when(j >= na)
    def _():
        ob_ref[...] = _dot(u_ref[...], w_ref[...])


def _in_proj(x, gain, w, n_a, w_dt, w_dt_t, tm, tn):
    m, d = x.shape
    n = w.shape[1]
    na = n_a // tn
    nh = w_dt_t.shape[0]
    return pl.pallas_call(
        functools.partial(_in_proj_kernel, na=na), grid=(m // tm, n // tn),
        in_specs=[pl.BlockSpec((tm, d), lambda i, j: (i, 0)),
                  pl.BlockSpec((1, d), lambda i, j: (0, 0)),
                  pl.BlockSpec((d, tn), lambda i, j: (0, j)),
                  pl.BlockSpec((d, LANES), lambda i, j: (0, 0)),
                  pl.BlockSpec((nh, d), lambda i, j: (0, 0))],
        out_specs=[pl.BlockSpec((tm, tn), lambda i, j: (i, jnp.minimum(j, na - 1))),
                   pl.BlockSpec((tm, tn), lambda i, j: (i, jnp.maximum(j - na, 0))),
                   pl.BlockSpec((tm, LANES), lambda i, j: (i, 0)),
                   pl.BlockSpec((nh, tm), lambda i, j: (0, i))],
        out_shape=[jax.ShapeDtypeStruct((m, n_a), BF16),
                   jax.ShapeDtypeStruct((m, n - n_a), F32),
                   jax.ShapeDtypeStruct((m, LANES), F32),
                   jax.ShapeDtypeStruct((nh, m), F32)],
        scratch_shapes=[pltpu.VMEM((tm, d), BF16)],
        compiler_params=pltpu.CompilerParams(dimension_semantics=("arbitrary", "arbitrary"),
                                             vmem_limit_bytes=VMEM_LIMIT),
        name="in_proj",
    )(x, gain, w, w_dt, w_dt_t)


def _sb_kernel(q_ref, k_ref, v_ref, mm_ref, o_ref, qs_ref, kc_ref, vc_ref, acc_ref, c_ref, *, nsub):
    qi = pl.program_id(2)
    blk = SB_BLOCK
    nblk = v_ref.shape[1] // blk
    head0 = lax.broadcasted_iota(jnp.int32, (blk, LANES), 1) < SB_HEAD_DIM
    key = jnp.bitwise_and(lax.broadcasted_iota(jnp.int32, (blk, 2 * blk), 1), blk - 1)
    causal = key < lax.broadcasted_iota(jnp.int32, (blk, 2 * blk), 0)
    scale = SB_HEAD_DIM ** -0.5

    def stack_heads(x):
        x = x.astype(F32)
        return jnp.concatenate([jnp.where(head0, x, 0.0), jnp.where(head0, 0.0, x)], axis=0).astype(BF16)

    @pl.when(qi == 0)
    def _():
        def fill(j, carry):
            off = pl.multiple_of(j * blk, blk)
            kc_ref[j] = stack_heads(k_ref[0, pl.ds(off, blk), :])
            vc_ref[j] = stack_heads(v_ref[0, pl.ds(off, blk), :])
            return carry
        lax.fori_loop(0, nblk, fill, 0)

    qs_ref[...] = (q_ref[0].astype(F32) * scale).astype(BF16)

    def step(s, mode):
        diag = mode == "diag"
        js = [qi * nsub + t - s for t in range(nsub)]
        jcs = [jnp.maximum(j, 0) for j in js] if mode == "edge" else js
        zs = [_dot_nt(qs_ref[t * blk:(t + 1) * blk, :], kc_ref[jcs[t]]) for t in range(nsub)]
        log_betas, sums = [], []
        for t in range(nsub):
            z = zs[t]
            sp = jnp.log(1.0 + jnp.exp2(jnp.abs(z) * -LOG2E))
            log_beta = jnp.minimum(z, 0.0) - sp
            log_keep = log_beta - z
            if diag:
                log_keep = jnp.where(causal, log_keep, 0.0)
            log_betas.append(log_beta)
            hi, lo = _split_bf16(log_keep, 2)
            sums.append([_dot(jnp.concatenate([hi[:, h * blk:(h + 1) * blk], lo[:, h * blk:(h + 1) * blk]], axis=1),
                              mm_ref[...]) for h in range(2)])
        cmax = None
        for t in range(nsub):
            r0, r1 = sums[t]
            later = jnp.concatenate([r0[:, :blk], r1[:, :blk]], axis=1)
            total = jnp.concatenate([r0[:, blk:], r1[:, blk:]], axis=1)
            if diag:
                w = jnp.where(causal, jnp.exp(log_betas[t] + later), 0.0)
                c_new = total
            else:
                c = c_ref[t]
                if mode == "edge":
                    c = jnp.where(js[t] >= 0, c, -SB_NO_KEYS)
                w = jnp.exp(log_betas[t] + later + c)
                c_new = c + total
            pv = _dot(w.astype(BF16), vc_ref[jcs[t]])
            if diag:
                acc_ref[t] = pv
            else:
                acc_ref[t] += pv
            c_ref[t] = c_new
            if not diag:
                cmax = c_new if cmax is None else jnp.maximum(cmax, c_new)
        return jnp.float32(0.0) if diag else jnp.max(cmax)

    def run(mode, last, state):
        def cond(st):
            s, cm = st
            return jnp.logical_and(s <= last, cm > -SB_EXP_UNDERFLOW)

        def body(st):
            return st[0] + 1, step(st[0], mode)

        return lax.while_loop(cond, body, state)

    state = (jnp.int32(1), step(0, "diag"))
    state = run("full", qi * nsub, state)
    run("edge", qi * nsub + nsub - 1, state)
    for t in range(nsub):
        o_ref[0, t * blk:(t + 1) * blk, :] = acc_ref[t].astype(o_ref.dtype)


def _sb_attention(qkv, b, s, tq):
    pairs = SB_HEADS * SB_HEAD_DIM // LANES
    nsub = tq // SB_BLOCK
    idx = np.arange(SB_BLOCK)
    later = (idx[:, None] > idx[None, :]).astype(np.float32)
    m = np.concatenate([later, np.ones((SB_BLOCK, SB_BLOCK), np.float32)], axis=1)
    mm = jnp.asarray(np.concatenate([m, m], axis=0), BF16)
    return pl.pallas_call(
        functools.partial(_sb_kernel, nsub=nsub),
        grid=(b, pairs, s // tq),
        in_specs=[
            pl.BlockSpec((1, tq, LANES), lambda bi, hp, qi: (bi, qi, hp)),
            pl.BlockSpec((1, s, LANES), lambda bi, hp, qi: (bi, 0, pairs + hp)),
            pl.BlockSpec((1, s, LANES), lambda bi, hp, qi: (bi, 0, 2 * pairs + hp)),
            pl.BlockSpec((2 * SB_BLOCK, 2 * SB_BLOCK), lambda bi, hp, qi: (0, 0)),
        ],
        out_specs=pl.BlockSpec((1, tq, LANES), lambda bi, hp, qi: (bi, qi, hp)),
        out_shape=jax.ShapeDtypeStruct((b, s, SB_HEADS * SB_HEAD_DIM), BF16),
        scratch_shapes=[pltpu.VMEM((tq, LANES), BF16),
                        pltpu.VMEM((s // SB_BLOCK, 2 * SB_BLOCK, LANES), BF16),
                        pltpu.VMEM((s // SB_BLOCK, 2 * SB_BLOCK, LANES), BF16),
                        pltpu.VMEM((nsub, SB_BLOCK, LANES), F32),
                        pltpu.VMEM((nsub, SB_BLOCK, 2 * SB_BLOCK), F32)],
        compiler_params=pltpu.CompilerParams(
            dimension_semantics=("arbitrary", "arbitrary", "arbitrary"), vmem_limit_bytes=VMEM_LIMIT),
        name="sb_attention",
    )(qkv, qkv, qkv, mm)


def _ssd_kernel(z_ref, xs_ref, bc_ref, dt_ref, dtt_ref,
                cwx_ref, cbx_ref, cwb_ref, cbb_ref,
                dtb_ref, dtbt_ref, alog_ref, alogt_ref, dexp_ref, gn_ref,
                tri_ref, trit_ref, e_ref,
                o_ref, bufx_ref, bufb_ref, state_ref, ydiag_ref):
    ck = SSD_CHUNK
    inner = SSD_HEADS * SSD_HEAD_DIM
    gw = inner // SSD_GROUPS
    gs = SSD_GROUPS * SSD_STATE

    @pl.when(pl.program_id(1) == 0)
    def _():
        bufx_ref[0:8, :] = jnp.zeros((8, inner), F32)
        bufb_ref[0:8, :] = jnp.zeros((8, 2 * gs), F32)
        state_ref[...] = jnp.zeros_like(state_ref)

    def conv_silu(x_ref, buf_ref, w_ref, b_ref):
        buf_ref[8:8 + ck, :] = x_ref[0]
        acc = b_ref[...]
        for k in range(SSD_CONV):
            acc = acc + buf_ref[8 - k:8 - k + ck, :] * w_ref[SSD_CONV - 1 - k:SSD_CONV - k, :]
        buf_ref[0:8, :] = buf_ref[ck:ck + 8, :]
        return acc * _sigmoid(acc)

    xs = conv_silu(xs_ref, bufx_ref, cwx_ref, cbx_ref)
    bcm = conv_silu(bc_ref, bufb_ref, cwb_ref, cbb_ref)

    dt = _softplus(dt_ref[...] + dtb_ref[...])
    dtt = _softplus(dtt_ref[...] + dtbt_ref[...])
    da = dt * (-jnp.exp(alog_ref[...]))
    dat = dtt * (-jnp.exp(alogt_ref[...]))
    a_cs = _dot_split_rhs(tri_ref[...], da, 3)
    a_cst = _dot_split_lhs(dat, trit_ref[...], 3)
    a_last = a_cs[ck - 1:ck, :]

    e = e_ref[...]
    dt_x = _dot_split_lhs(dt, e, 2)
    ea_x = _dot_split_lhs(jnp.exp(a_cs), e, 2)
    dte_x = _dot_split_lhs(jnp.exp(a_last - a_cs), e, 2)

    x_dt = xs * dt_x
    x_dt16 = x_dt.astype(BF16)
    x_end16 = (x_dt * dte_x).astype(BF16)

    lane = lax.broadcasted_iota(jnp.int32, (ck, LANES), 1)
    row = lax.broadcasted_iota(jnp.int32, (ck, LANES), 0)
    head0 = lane < SSD_HEAD_DIM
    tri_mask = lane <= row

    heads_per_group = SSD_HEADS // SSD_GROUPS
    for g in range(SSD_GROUPS):
        bg = bcm[:, g * SSD_STATE:(g + 1) * SSD_STATE]
        cg16 = bcm[:, gs + g * SSD_STATE:gs + (g + 1) * SSD_STATE].astype(BF16)
        cb = _dot_nt(cg16, bg.astype(BF16))
        for rp in range(heads_per_group // 2):
            pair = g * (heads_per_group // 2) + rp
            xp = x_dt16[:, pair * LANES:(pair + 1) * LANES]
            ys = []
            for hh in range(2):
                r = 2 * pair + hh
                seg = a_cs[:, r:r + 1] - a_cst[r:r + 1, :]
                lmat = cb * jnp.exp(jnp.where(tri_mask, seg, -jnp.inf))
                ys.append(_dot(lmat.astype(BF16), xp))
            ydiag_ref[:, pair * LANES:(pair + 1) * LANES] = jnp.where(head0, ys[0], ys[1])

        cols = slice(g * gw, (g + 1) * gw)
        st = state_ref[:, cols]
        y_off = _dot(cg16, st.astype(BF16)) * ea_x[:, cols]
        contrib = _dot(bg.T.astype(BF16), x_end16[:, cols])
        state_ref[:, cols] = st * ea_x[ck - 1:ck, cols] + contrib

        y = ydiag_ref[:, cols] + y_off + dexp_ref[:, cols] * xs[:, cols]
        zg = z_ref[0, :, cols]
        y = y * (zg * _sigmoid(zg))
        ms = jnp.mean(y * y, axis=-1, keepdims=True)
        o_ref[0, :, cols] = (y * lax.rsqrt(ms + EPS) * gn_ref[:, cols]).astype(o_ref.dtype)


def _ssd_branch(proj, dt, dtt, b, s, conv_w, conv_b, dt_bias, a_log, d_skip, ssd_norm):
    inner = SSD_HEADS * SSD_HEAD_DIM
    gs = SSD_GROUPS * SSD_STATE
    nc = s // SSD_CHUNK
    idx = np.arange(SSD_CHUNK)
    tri = jnp.asarray((idx[None, :] <= idx[:, None]).astype(np.float32), BF16)
    trit = jnp.asarray((idx[:, None] <= idx[None, :]).astype(np.float32), BF16)
    expand = np.zeros((LANES, inner), np.float32)
    expand[np.arange(inner) // SSD_HEAD_DIM, np.arange(inner)] = 1.0
    expand = jnp.asarray(expand, BF16)

    pad = LANES - SSD_HEADS
    row = lambda v: v.reshape(1, -1)
    dtb = jnp.pad(row(dt_bias), ((0, 0), (0, pad)))
    alog = jnp.pad(row(a_log), ((0, 0), (0, pad)))
    dexp = row(jnp.repeat(d_skip, SSD_HEAD_DIM))

    const = lambda shape: pl.BlockSpec(shape, lambda bi, ci: (0,) * len(shape))
    return pl.pallas_call(
        _ssd_kernel, grid=(b, nc),
        in_specs=[
            pl.BlockSpec((1, SSD_CHUNK, inner), lambda bi, ci: (bi, ci, 0)),
            pl.BlockSpec((1, SSD_CHUNK, inner), lambda bi, ci: (bi, ci, 1)),
            pl.BlockSpec((1, SSD_CHUNK, 2 * gs), lambda bi, ci: (bi, ci, 2 * inner // (2 * gs))),
            pl.BlockSpec((SSD_CHUNK, LANES), lambda bi, ci: (bi * nc + ci, 0)),
            pl.BlockSpec((SSD_HEADS, SSD_CHUNK), lambda bi, ci: (0, bi * nc + ci)),
            const((SSD_CONV, inner)), const((1, inner)),
            const((SSD_CONV, 2 * gs)), const((1, 2 * gs)),
            const((1, LANES)), const((SSD_HEADS, 1)), const((1, LANES)), const((SSD_HEADS, 1)),
            const((1, inner)), const((1, inner)),
            const((SSD_CHUNK, SSD_CHUNK)), const((SSD_CHUNK, SSD_CHUNK)), const((LANES, inner)),
        ],
        out_specs=pl.BlockSpec((1, SSD_CHUNK, inner), lambda bi, ci: (bi, ci, 0)),
        out_shape=jax.ShapeDtypeStruct((b, s, inner), BF16),
        scratch_shapes=[pltpu.VMEM((8 + SSD_CHUNK, inner), F32), pltpu.VMEM((8 + SSD_CHUNK, 2 * gs), F32),
                        pltpu.VMEM((SSD_STATE, inner), F32), pltpu.VMEM((SSD_CHUNK, inner), F32)],
        compiler_params=pltpu.CompilerParams(
            dimension_semantics=("arbitrary", "arbitrary"), vmem_limit_bytes=VMEM_LIMIT),
        name="ssd_branch",
    )(proj, proj, proj, dt, dtt,
      conv_w[:, :inner], row(conv_b[:inner]), conv_w[:, inner:], row(conv_b[inner:]),
      dtb, dt_bias.reshape(-1, 1), alog, a_log.reshape(-1, 1), dexp, row(ssd_norm),
      tri, trit, expand)


def _mem_attn_kernel(q_ref, kv_ref, o_ref):
    width = MEM_HEADS * MEM_HEAD_DIM
    scale = MEM_HEAD_DIM ** -0.5
    for h in range(MEM_HEADS):
        cols = slice(h * MEM_HEAD_DIM, (h + 1) * MEM_HEAD_DIM)
        q = q_ref[0, :, cols]
        k = kv_ref[0, :, cols]
        v = kv_ref[0, :, width + h * MEM_HEAD_DIM:width + (h + 1) * MEM_HEAD_DIM]
        sc = _dot_nt(q, k) * scale
        p = jnp.exp(sc - jnp.max(sc, axis=-1, keepdims=True))
        probs = p / jnp.sum(p, axis=-1, keepdims=True)
        o_ref[0, :, cols] = _dot(probs.astype(BF16), v).astype(o_ref.dtype)


def _mem_attention(qkv, kv, b, s, tq):
    width = MEM_HEADS * MEM_HEAD_DIM
    mlen = kv.shape[1]
    return pl.pallas_call(
        _mem_attn_kernel, grid=(b, s // tq),
        in_specs=[pl.BlockSpec((1, tq, width), lambda bi, qi: (bi, qi, 3)),
                  pl.BlockSpec((1, mlen, 2 * width), lambda bi, qi: (bi, 0, 0))],
        out_specs=pl.BlockSpec((1, tq, width), lambda bi, qi: (bi, qi, 0)),
        out_shape=jax.ShapeDtypeStruct((b, s, width), BF16),
        compiler_params=pltpu.CompilerParams(
            dimension_semantics=("arbitrary", "arbitrary"), vmem_limit_bytes=VMEM_LIMIT),
        name="mem_attention",
    )(qkv, kv)


def _merge_kernel(x_ref, ysb_ref, yssd_ref, ymem_ref, g0_ref, g1_ref, g2_ref,
                  wsb_ref, wssd_ref, wmem_ref, wo_ref, gain_ref, o_ref):
    merged = (_sigmoid(g0_ref[...]) * _dot(ysb_ref[...], wsb_ref[...])
              + _sigmoid(g1_ref[...]) * _dot(yssd_ref[...], wssd_ref[...])
              + _sigmoid(g2_ref[...]) * _dot(ymem_ref[...], wmem_ref[...]))
    mix = _dot(merged.astype(BF16), wo_ref[...])
    o_ref[...] = x_ref[...] + _rms_rows(mix, gain_ref[...])


def _merge(x, y_sb, y_ssd, y_mem, proj, w_sb, w_ssd, w_mem, w_o, gain, tm):
    m, d = x.shape
    inner = y_ssd.shape[1]
    gate0 = proj.shape[1] // d - 3
    tile = lambda w: pl.BlockSpec((tm, w), lambda i: (i, 0))
    gate = lambda k: pl.BlockSpec((tm, d), lambda i: (i, gate0 + k))
    full = lambda a: pl.BlockSpec(a.shape, lambda i: (0, 0))
    return pl.pallas_call(
        _merge_kernel, grid=(m // tm,),
        in_specs=[tile(d), tile(d), tile(inner), tile(d), gate(0), gate(1), gate(2),
                  full(w_sb), full(w_ssd), full(w_mem), full(w_o), full(gain)],
        out_specs=tile(d),
        out_shape=jax.ShapeDtypeStruct((m, d), F32),
        compiler_params=pltpu.CompilerParams(
            dimension_semantics=("arbitrary",), vmem_limit_bytes=VMEM_LIMIT),
        name="merge",
    )(x, y_sb, y_ssd, y_mem, proj, proj, proj, w_sb, w_ssd, w_mem, w_o, gain)


def _mlp_kernel(h_ref, gpre_ref, wup_ref, wdown_ref, gpost_ref, o_ref, *, chunk):
    h = h_ref[...]
    u = _rms_rows(h, gpre_ref[...]).astype(BF16)
    d_ff = wup_ref.shape[1]
    ff = None
    for c in range(d_ff // chunk):
        hid = _dot(u, wup_ref[:, c * chunk:(c + 1) * chunk])
        act = jnp.square(jnp.maximum(hid, 0.0)).astype(BF16)
        t = _dot(act, wdown_ref[c * chunk:(c + 1) * chunk, :])
        ff = t if ff is None else ff + t
    o_ref[...] = h + _rms_rows(ff, gpost_ref[...])


def _mlp(h, g_pre, w_up, w_down, g_post, tm):
    m, d = h.shape
    full = lambda a: pl.BlockSpec(a.shape, lambda i: (0, 0))
    tile = pl.BlockSpec((tm, d), lambda i: (i, 0))
    return pl.pallas_call(
        functools.partial(_mlp_kernel, chunk=1024), grid=(m // tm,),
        in_specs=[tile, full(g_pre), full(w_up), full(w_down), full(g_post)],
        out_specs=tile,
        out_shape=jax.ShapeDtypeStruct((m, d), F32),
        compiler_params=pltpu.CompilerParams(
            dimension_semantics=("arbitrary",), vmem_limit_bytes=VMEM_LIMIT),
        name="mlp",
    )(h, g_pre, w_up, w_down, g_post)


def _layer(h, mem, norm_mix_pre, w_in, conv_w, conv_b, dt_bias, a_log, d_skip, ssd_norm,
           norm_mem, w_mem_kv, w_sb_out, w_ssd_out, w_mem_out, w_o, norm_mix_post,
           norm_mlp_pre, w_up, w_down, norm_mlp_post):
    b, s, d = h.shape
    m = b * s
    row = lambda v: v.reshape(1, -1)
    x2 = h.reshape(m, d)

    sb_w = 3 * SB_HEADS * SB_HEAD_DIM
    inner = SSD_HEADS * SSD_HEAD_DIM
    conv_dim = inner + 2 * SSD_GROUPS * SSD_STATE
    o_z, o_xbc, o_dt = sb_w, sb_w + inner, sb_w + inner + conv_dim
    o_memq = o_dt + SSD_HEADS
    o_gate = o_memq + MEM_HEADS * MEM_HEAD_DIM

    w_ab = jnp.concatenate([w_in[:, :sb_w], w_in[:, o_memq:o_gate],
                            w_in[:, o_z:o_dt], w_in[:, o_gate:]], axis=1).astype(BF16)
    w_dt = w_in[:, o_dt:o_memq]
    w_dt_pad = jnp.pad(w_dt, ((0, 0), (0, LANES - SSD_HEADS))).astype(BF16)
    w_dt_t = w_dt.T.astype(BF16)

    qkv, proj, dt, dtt = _in_proj(x2, row(norm_mix_pre), w_ab, sb_w + MEM_HEADS * MEM_HEAD_DIM,
                                  w_dt_pad, w_dt_t, 1024, 1024)
    qkv = qkv.reshape(b, s, -1)

    y_sb = _sb_attention(qkv, b, s, 1024)
    y_ssd = _ssd_branch(proj.reshape(b, s, -1), dt, dtt, b, s,
                        conv_w, conv_b, dt_bias, a_log, d_skip, ssd_norm)

    mlen = mem.shape[1]
    kv = _norm_proj(mem.reshape(b * mlen, d), row(norm_mem), w_mem_kv.astype(BF16), BF16, b * mlen, 1024)
    y_mem = _mem_attention(qkv, kv.reshape(b, mlen, -1), b, s, 512)

    h1 = _merge(x2, y_sb.reshape(m, -1), y_ssd.reshape(m, -1), y_mem.reshape(m, -1), proj,
                w_sb_out.astype(BF16), w_ssd_out.astype(BF16), w_mem_out.astype(BF16),
                w_o.astype(BF16), row(norm_mix_post), 256)
    out = _mlp(h1, row(norm_mlp_pre), w_up.astype(BF16), w_down.astype(BF16), row(norm_mlp_post), 256)
    return out.reshape(b, s, d)


def kernel(x, mem, norm_mix_pre, w_in, conv_w, conv_b, dt_bias, a_log, d_skip, ssd_norm, norm_mem, w_mem_kv, w_sb_out, w_ssd_out, w_mem_out, w_o, norm_mix_post, norm_mlp_pre, w_up, w_down, norm_mlp_post):
    h = x
    for layer in range(w_in.shape[0]):
        h = _layer(h, mem, norm_mix_pre[layer], w_in[layer], conv_w[layer], conv_b[layer],
                   dt_bias[layer], a_log[layer], d_skip[layer], ssd_norm[layer], norm_mem[layer],
                   w_mem_kv[layer], w_sb_out[layer], w_ssd_out[layer], w_mem_out[layer], w_o[layer],
                   norm_mix_post[layer], norm_mlp_pre[layer], w_up[layer], w_down[layer],
                   norm_mlp_post[layer])
    return h
```

```python
import functools

import jax
import jax.numpy as jnp
import numpy as np
from jax import lax
from jax.experimental import pallas as pl
from jax.experimental.pallas import tpu as pltpu

F32 = jnp.float32
BF16 = jnp.bfloat16

EPS = 1e-6
LOG2E = 1.4426950408889634
LANES = 128
SB_HEADS = 16
SB_HEAD_DIM = 64
SB_BLOCK = 128
SSD_HEADS = 32
SSD_HEAD_DIM = 64
SSD_GROUPS = 4
SSD_STATE = 128
SSD_CHUNK = 128
SSD_CONV = 4
MEM_HEADS = 4
MEM_HEAD_DIM = 256
N_GATES = 3
VMEM_LIMIT = 56 * 1024 * 1024

SB_EXP_UNDERFLOW = 110.0
SB_NO_KEYS = 1e30


SSD_STRIDE = 4


def _ssd_row_time(pos):
    span = 8 * SSD_STRIDE
    return (pos & -span) + (pos & 7) * SSD_STRIDE + ((pos >> 3) & (SSD_STRIDE - 1))


def _dot(a, b):
    return jnp.dot(a, b, preferred_element_type=F32)


def _dot_nt(a, b):
    return lax.dot_general(a, b, (((1,), (1,)), ((), ())), preferred_element_type=F32)


def _split_bf16(x, parts):
    out = []
    rem = x
    for _ in range(parts):
        p = rem.astype(BF16)
        out.append(p)
        rem = rem - p.astype(F32)
    return out


def _dot_split_lhs(x, m, parts):
    acc = None
    for p in _split_bf16(x, parts):
        t = _dot(p, m)
        acc = t if acc is None else acc + t
    return acc


def _dot_split_rhs(m, x, parts):
    acc = None
    for p in _split_bf16(x, parts):
        t = _dot(m, p)
        acc = t if acc is None else acc + t
    return acc


def _rms_rows(x, gain):
    ms = jnp.mean(x * x, axis=-1, keepdims=True)
    return x * lax.rsqrt(ms + EPS) * gain


def _softplus(x):
    return jnp.maximum(x, 0.0) + jnp.log1p(jnp.exp(-jnp.abs(x)))


def _sigmoid(x):
    return 1.0 / (1.0 + jnp.exp2(x * -LOG2E))


def _silu(x):
    h = 0.5 * x
    return h + h * jnp.tanh(h)


def _norm_proj_kernel(x_ref, g_ref, w_ref, o_ref, u_ref):
    @pl.when(pl.program_id(1) == 0)
    def _():
        u_ref[...] = _rms_rows(x_ref[...], g_ref[...]).astype(BF16)

    o_ref[...] = _dot(u_ref[...], w_ref[...]).astype(o_ref.dtype)


def _norm_proj(x, gain, w, out_dtype, tm, tn):
    m, d = x.shape
    n = w.shape[1]
    return pl.pallas_call(
        _norm_proj_kernel, grid=(m // tm, n // tn),
        in_specs=[pl.BlockSpec((tm, d), lambda i, j: (i, 0)),
                  pl.BlockSpec((1, d), lambda i, j: (0, 0)),
                  pl.BlockSpec((d, tn), lambda i, j: (0, j))],
        out_specs=pl.BlockSpec((tm, tn), lambda i, j: (i, j)),
        out_shape=jax.ShapeDtypeStruct((m, n), out_dtype),
        scratch_shapes=[pltpu.VMEM((tm, d), BF16)],
        compiler_params=pltpu.CompilerParams(dimension_semantics=("arbitrary", "arbitrary"),
                                             vmem_limit_bytes=VMEM_LIMIT),
        name="norm_proj",
    )(x, gain, w)


def _in_proj_kernel(x_ref, g_ref, wt_ref, wdtt_ref, oa_ref, ob_ref, oc_ref, dt_ref, dtt_ref, u_ref, *, kinds, nh):
    j = pl.program_id(1)

    @pl.when(j == 0)
    def _():
        u = _rms_rows(x_ref[...], g_ref[...]).astype(BF16)
        u_ref[...] = u
        dt_ref[...] = _dot_nt(u, wdtt_ref[...])
        dtt_ref[...] = _dot_nt(wdtt_ref[0:nh, :], u)

    def tile_is(kind):
        return functools.reduce(jnp.logical_or, [j == t for t, k in enumerate(kinds) if k == kind])

    @pl.when(tile_is("a"))
    def _():
        oa_ref[...] = _dot_nt(u_ref[...], wt_ref[...]).astype(oa_ref.dtype)

    @pl.when(tile_is("b"))
    def _():
        ob_ref[...] = _dot_nt(u_ref[...], wt_ref[...])

    @pl.when(tile_is("c"))
    def _():
        res = _dot_nt(u_ref[...], wt_ref[...])
        for r in range(oc_ref.shape[0]):
            for c in range(oc_ref.shape[1]):
                oc_ref[r, c] = res[r * LANES:(r + 1) * LANES, c * LANES:(c + 1) * LANES]


def _in_proj(x, gain, w_t, skip, kinds, w_dt_t, nh, tm, tn):
    m, d = x.shape
    nt = (w_t.shape[0] - (skip[1] - skip[0])) // tn
    assert len(kinds) == nt
    skip_tile, skip_rows = skip[0] // tn, skip[1] - skip[0]
    count = {k: sum(1 for q in kinds if q == k) for k in "abc"}

    def rank(kind, j):
        tiles = [t for t, k in enumerate(kinds) if k == kind]
        return jnp.maximum(sum((j >= t).astype(jnp.int32) for t in tiles) - 1, 0)

    return pl.pallas_call(
        functools.partial(_in_proj_kernel, kinds=tuple(kinds), nh=nh), grid=(m // tm, nt),
        in_specs=[pl.BlockSpec((tm, d), lambda i, j: (i, 0)),
                  pl.BlockSpec((1, d), lambda i, j: (0, 0)),
                  pl.BlockSpec((pl.Element(tn), pl.Element(d)),
                               lambda i, j: ((j * (tn // skip_rows) + (j >= skip_tile).astype(jnp.int32)) * skip_rows, 0)),
                  pl.BlockSpec((LANES, d), lambda i, j: (0, 0))],
        out_specs=[pl.BlockSpec((tm, tn), lambda i, j: (i, rank("a", j))),
                   pl.BlockSpec((tm, tn), lambda i, j: (i, rank("b", j))),
                   pl.BlockSpec((tm // LANES, tn // LANES, LANES, LANES), lambda i, j: (i, rank("c", j), 0, 0)),
                   pl.BlockSpec((tm, LANES), lambda i, j: (i, 0)),
                   pl.BlockSpec((nh, tm), lambda i, j: (0, i))],
        out_shape=[jax.ShapeDtypeStruct((m, count["a"] * tn), BF16),
                   jax.ShapeDtypeStruct((m, count["b"] * tn), F32),
                   jax.ShapeDtypeStruct((m // LANES, count["c"] * tn // LANES, LANES, LANES), F32),
                   jax.ShapeDtypeStruct((m, LANES), F32),
                   jax.ShapeDtypeStruct((nh, m), F32)],
        scratch_shapes=[pltpu.VMEM((tm, d), BF16)],
        compiler_params=pltpu.CompilerParams(dimension_semantics=("arbitrary", "arbitrary"),
                                             vmem_limit_bytes=VMEM_LIMIT),
        name="in_proj",
    )(x, gain, w_t, w_dt_t)


def _sb_kernel(q_ref, k_ref, v_ref, mm_ref, o_ref, qs_ref, kc_ref, vc_ref, acc_ref, c_ref, *, nsub):
    qi = pl.program_id(2)
    blk = SB_BLOCK
    nblk = v_ref.shape[1] // blk
    head0 = lax.broadcasted_iota(jnp.int32, (blk, LANES), 1) < SB_HEAD_DIM
    key = jnp.bitwise_and(lax.broadcasted_iota(jnp.int32, (blk, 2 * blk), 1), blk - 1)
    causal = key < lax.broadcasted_iota(jnp.int32, (blk, 2 * blk), 0)
    scale = SB_HEAD_DIM ** -0.5

    def stack_heads(x):
        x = x.astype(F32)
        return jnp.concatenate([jnp.where(head0, x, 0.0), jnp.where(head0, 0.0, x)], axis=0).astype(BF16)

    @pl.when(qi == 0)
    def _():
        def fill(j, carry):
            off = pl.multiple_of(j * blk, blk)
            kc_ref[j] = stack_heads(k_ref[0, pl.ds(off, blk), :])
            vc_ref[j] = stack_heads(v_ref[0, pl.ds(off, blk), :])
            return carry
        lax.fori_loop(0, nblk, fill, 0)

    qs_ref[...] = (q_ref[0].astype(F32) * scale).astype(BF16)

    def step(s, mode):
        diag = mode == "diag"
        js = [qi * nsub + t - s for t in range(nsub)]
        jcs = [jnp.maximum(j, 0) for j in js] if mode == "edge" else js
        zs = [_dot_nt(qs_ref[t * blk:(t + 1) * blk, :], kc_ref[jcs[t]]) for t in range(nsub)]
        log_betas, sums = [], []
        for t in range(nsub):
            z = zs[t]
            sp = jnp.log(1.0 + jnp.exp2(jnp.abs(z) * -LOG2E))
            log_beta = jnp.minimum(z, 0.0) - sp
            log_keep = log_beta - z
            if diag:
                log_keep = jnp.where(causal, log_keep, 0.0)
            log_betas.append(log_beta)
            hi, lo = _split_bf16(log_keep, 2)
            sums.append([_dot(jnp.concatenate([hi[:, h * blk:(h + 1) * blk], lo[:, h * blk:(h + 1) * blk]], axis=1),
                              mm_ref[...]) for h in range(2)])
        cmax = None
        for t in range(nsub):
            r0, r1 = sums[t]
            later = jnp.concatenate([r0[:, :blk], r1[:, :blk]], axis=1)
            total = jnp.concatenate([r0[:, blk:], r1[:, blk:]], axis=1)
            if diag:
                w = jnp.where(causal, jnp.exp(log_betas[t] + later), 0.0)
                c_new = total
            else:
                c = c_ref[t]
                if mode == "edge":
                    c = jnp.where(js[t] >= 0, c, -SB_NO_KEYS)
                w = jnp.exp(log_betas[t] + later + c)
                c_new = c + total
            pv = _dot(w.astype(BF16), vc_ref[jcs[t]])
            if diag:
                acc_ref[t] = pv
            else:
                acc_ref[t] += pv
            c_ref[t] = c_new
            if not diag:
                cmax = c_new if cmax is None else jnp.maximum(cmax, c_new)
        return jnp.float32(0.0) if diag else jnp.max(cmax)

    def run(mode, last, state):
        def cond(st):
            s, cm = st
            return jnp.logical_and(s <= last, cm > -SB_EXP_UNDERFLOW)

        def body(st):
            return st[0] + 1, step(st[0], mode)

        return lax.while_loop(cond, body, state)

    state = (jnp.int32(1), step(0, "diag"))
    state = run("full", qi * nsub, state)
    run("edge", qi * nsub + nsub - 1, state)
    for t in range(nsub):
        o_ref[0, t * blk:(t + 1) * blk, :] = acc_ref[t].astype(o_ref.dtype)


def _sb_attention(qkv, b, s, tq):
    pairs = SB_HEADS * SB_HEAD_DIM // LANES
    nsub = tq // SB_BLOCK
    idx = np.arange(SB_BLOCK)
    later = (idx[:, None] > idx[None, :]).astype(np.float32)
    m = np.concatenate([later, np.ones((SB_BLOCK, SB_BLOCK), np.float32)], axis=1)
    mm = jnp.asarray(np.concatenate([m, m], axis=0), BF16)
    return pl.pallas_call(
        functools.partial(_sb_kernel, nsub=nsub),
        grid=(b, pairs, s // tq),
        in_specs=[
            pl.BlockSpec((1, tq, LANES), lambda bi, hp, qi: (bi, qi, hp)),
            pl.BlockSpec((1, s, LANES), lambda bi, hp, qi: (bi, 0, pairs + hp)),
            pl.BlockSpec((1, s, LANES), lambda bi, hp, qi: (bi, 0, 2 * pairs + hp)),
            pl.BlockSpec((2 * SB_BLOCK, 2 * SB_BLOCK), lambda bi, hp, qi: (0, 0)),
        ],
        out_specs=pl.BlockSpec((1, tq, LANES), lambda bi, hp, qi: (bi, qi, hp)),
        out_shape=jax.ShapeDtypeStruct((b, s, SB_HEADS * SB_HEAD_DIM), BF16),
        scratch_shapes=[pltpu.VMEM((tq, LANES), BF16),
                        pltpu.VMEM((s // SB_BLOCK, 2 * SB_BLOCK, LANES), BF16),
                        pltpu.VMEM((s // SB_BLOCK, 2 * SB_BLOCK, LANES), BF16),
                        pltpu.VMEM((nsub, SB_BLOCK, LANES), F32),
                        pltpu.VMEM((nsub, SB_BLOCK, 2 * SB_BLOCK), F32)],
        compiler_params=pltpu.CompilerParams(
            dimension_semantics=("arbitrary", "arbitrary", "arbitrary"), vmem_limit_bytes=VMEM_LIMIT),
        name="sb_attention",
    )(qkv, qkv, qkv, mm)


def _ssd_kernel(z_ref, xc_ref, dt_ref, dtt_ref, cw_ref, cb_ref,
                dtb_ref, dtbt_ref, alog_ref, alogt_ref, dexp_ref, gn_ref,
                tri_ref, trit_ref, e_ref,
                o_ref, prev_ref, state_ref, ydiag_ref, yout_ref):
    ck = SSD_CHUNK
    st = SSD_STRIDE
    span = 8 * st
    inner = SSD_HEADS * SSD_HEAD_DIM
    gw = inner // SSD_GROUPS
    gs = SSD_GROUPS * SSD_STATE
    tile_rows = [pl.ds(g * span + i, 8, stride=st) for g in range(ck // span) for i in range(st)]

    @pl.when(pl.program_id(1) == 0)
    def _():
        prev_ref[...] = jnp.zeros_like(prev_ref)
        state_ref[...] = jnp.zeros_like(state_ref)

    first_row = lax.broadcasted_iota(jnp.int32, (8, LANES), 0) == 0

    def conv_silu(c):
        lanes = slice(c * LANES, (c + 1) * LANES)
        tiles = [xc_ref[0, c, rows, :] for rows in tile_rows]
        down = {-j: pltpu.roll(prev_ref[j - 1, :, lanes], 1, 0) for j in range(1, SSD_CONV)}
        for n, tile in enumerate(tiles):
            if n % st >= st - (SSD_CONV - 1):
                down[n] = pltpu.roll(tile, 1, 0)
        wrapped = {}

        def back(n, k):
            if n % st >= k:
                return tiles[n - k]
            if n - k not in wrapped:
                wrapped[n - k] = jnp.where(first_row, down[n - k], down[n - k + st])
            return wrapped[n - k]

        out = []
        for n in range(len(tiles)):
            acc = cb_ref[:, lanes]
            for k in range(SSD_CONV):
                acc = acc + back(n, k) * cw_ref[SSD_CONV - 1 - k:SSD_CONV - k, lanes]
            out.append(_silu(acc))
        for j in range(1, SSD_CONV):
            prev_ref[j - 1, :, lanes] = tiles[len(tiles) - j]
        return jnp.concatenate(out, axis=0)

    xs = jnp.concatenate([conv_silu(c) for c in range(inner // LANES)], axis=1)
    bcm = jnp.concatenate([conv_silu(c) for c in range(inner // LANES, (inner + 2 * gs) // LANES)], axis=1)

    dt_raw = jnp.concatenate([dt_ref[rows, :] for rows in tile_rows], axis=0)
    dt = _softplus(dt_raw + dtb_ref[...])
    dtt = _softplus(dtt_ref[...] + dtbt_ref[...])
    da = dt * (-jnp.exp(alog_ref[...]))
    dat = dtt * (-jnp.exp(alogt_ref[...]))
    a_cs = _dot_split_rhs(tri_ref[...], da, 3)
    a_cst = _dot_split_lhs(dat, trit_ref[...], 3)
    a_last = a_cs[ck - 1:ck, :]

    e = e_ref[...]
    dt_x = _dot_split_lhs(dt, e, 2)
    ea_x = _dot_split_lhs(jnp.exp(a_cs), e, 2)
    dte_x = _dot_split_lhs(jnp.exp(a_last - a_cs), e, 2)

    x_dt = xs * dt_x
    x_dt16 = x_dt.astype(BF16)
    x_end16 = (x_dt * dte_x).astype(BF16)

    lane = lax.broadcasted_iota(jnp.int32, (ck, LANES), 1)
    row = lax.broadcasted_iota(jnp.int32, (ck, LANES), 0)
    head0 = lane < SSD_HEAD_DIM
    tri_mask = _ssd_row_time(lane) <= _ssd_row_time(row)

    heads_per_group = SSD_HEADS // SSD_GROUPS
    for g in range(SSD_GROUPS):
        bg = bcm[:, g * SSD_STATE:(g + 1) * SSD_STATE]
        cg16 = bcm[:, gs + g * SSD_STATE:gs + (g + 1) * SSD_STATE].astype(BF16)
        cb = _dot_nt(cg16, bg.astype(BF16))
        for rp in range(heads_per_group // 2):
            pair = g * (heads_per_group // 2) + rp
            xp = x_dt16[:, pair * LANES:(pair + 1) * LANES]
            ys = []
            for hh in range(2):
                r = 2 * pair + hh
                seg = a_cs[:, r:r + 1] - a_cst[r:r + 1, :]
                lmat = cb * jnp.exp(jnp.where(tri_mask, seg, -jnp.inf))
                ys.append(_dot(lmat.astype(BF16), xp))
            ydiag_ref[:, pair * LANES:(pair + 1) * LANES] = jnp.where(head0, ys[0], ys[1])

        cols = slice(g * gw, (g + 1) * gw)
        st = state_ref[:, cols]
        y_off = _dot(cg16, st.astype(BF16)) * ea_x[:, cols]
        contrib = _dot(bg.T.astype(BF16), x_end16[:, cols])
        state_ref[:, cols] = st * ea_x[ck - 1:ck, cols] + contrib

        y = ydiag_ref[:, cols] + y_off + dexp_ref[:, cols] * xs[:, cols]
        chunks = range(g * gw // LANES, (g + 1) * gw // LANES)
        for c in chunks:
            for n, rows in enumerate(tile_rows):
                yout_ref[c, rows, :] = y[8 * n:8 * n + 8, (c - chunks[0]) * LANES:(c - chunks[0] + 1) * LANES]
        y = jnp.concatenate([yout_ref[c] for c in chunks], axis=1)
        zg = z_ref[0, :, cols]
        y = y * _silu(zg)
        ms = jnp.mean(y * y, axis=-1, keepdims=True)
        o_ref[0, :, cols] = (y * lax.rsqrt(ms + EPS) * gn_ref[:, cols]).astype(o_ref.dtype)


def _ssd_branch(proj, xbc, dt, dtt, b, s, conv_w, conv_b, dt_bias, a_log, d_skip, ssd_norm):
    inner = SSD_HEADS * SSD_HEAD_DIM
    gs = SSD_GROUPS * SSD_STATE
    nc = s // SSD_CHUNK
    ncol = (inner + 2 * gs) // LANES
    idx = np.arange(SSD_CHUNK)
    when = _ssd_row_time(idx)
    tri = jnp.asarray((when[None, :] <= when[:, None]).astype(np.float32), BF16)
    trit = jnp.asarray((idx[:, None] <= when[None, :]).astype(np.float32), BF16)
    expand = np.zeros((LANES, inner), np.float32)
    expand[np.arange(inner) // SSD_HEAD_DIM, np.arange(inner)] = 1.0
    expand = jnp.asarray(expand, BF16)

    pad = LANES - SSD_HEADS
    row = lambda v: v.reshape(1, -1)
    dtb = jnp.pad(row(dt_bias), ((0, 0), (0, pad)))
    alog = jnp.pad(row(a_log), ((0, 0), (0, pad)))
    dexp = row(jnp.repeat(d_skip, SSD_HEAD_DIM))

    const = lambda shape: pl.BlockSpec(shape, lambda bi, ci: (0,) * len(shape))
    return pl.pallas_call(
        _ssd_kernel, grid=(b, nc),
        in_specs=[
            pl.BlockSpec((1, SSD_CHUNK, inner), lambda bi, ci: (bi, ci, 0)),
            pl.BlockSpec((1, ncol, SSD_CHUNK, LANES), lambda bi, ci: (bi * nc + ci, 0, 0, 0)),
            pl.BlockSpec((SSD_CHUNK, LANES), lambda bi, ci: (bi * nc + ci, 0)),
            pl.BlockSpec((SSD_HEADS, SSD_CHUNK), lambda bi, ci: (0, bi * nc + ci)),
            const((SSD_CONV, inner + 2 * gs)), const((1, inner + 2 * gs)),
            const((1, LANES)), const((SSD_HEADS, 1)), const((1, LANES)), const((SSD_HEADS, 1)),
            const((1, inner)), const((1, inner)),
            const((SSD_CHUNK, SSD_CHUNK)), const((SSD_CHUNK, SSD_CHUNK)), const((LANES, inner)),
        ],
        out_specs=pl.BlockSpec((1, SSD_CHUNK, inner), lambda bi, ci: (bi, ci, 0)),
        out_shape=jax.ShapeDtypeStruct((b, s, inner), BF16),
        scratch_shapes=[pltpu.VMEM((SSD_CONV - 1, 8, inner + 2 * gs), F32),
                        pltpu.VMEM((SSD_STATE, inner), F32),
                        pltpu.VMEM((SSD_CHUNK, inner), F32),
                        pltpu.VMEM((inner // LANES, SSD_CHUNK, LANES), F32)],
        compiler_params=pltpu.CompilerParams(
            dimension_semantics=("arbitrary", "arbitrary"), vmem_limit_bytes=VMEM_LIMIT),
        name="ssd_branch",
    )(proj, xbc, dt, dtt, conv_w, row(conv_b),
      dtb, dt_bias.reshape(-1, 1), alog, a_log.reshape(-1, 1), dexp, row(ssd_norm),
      tri, trit, expand)


def _mem_attn_kernel(q_ref, kv_ref, o_ref):
    width = MEM_HEADS * MEM_HEAD_DIM
    scale = MEM_HEAD_DIM ** -0.5
    for h in range(MEM_HEADS):
        cols = slice(h * MEM_HEAD_DIM, (h + 1) * MEM_HEAD_DIM)
        q = q_ref[0, :, cols]
        k = kv_ref[0, :, cols]
        v = kv_ref[0, :, width + h * MEM_HEAD_DIM:width + (h + 1) * MEM_HEAD_DIM]
        sc = _dot_nt(q, k) * scale
        p = jnp.exp(sc - jnp.max(sc, axis=-1, keepdims=True))
        probs = p / jnp.sum(p, axis=-1, keepdims=True)
        o_ref[0, :, cols] = _dot(probs.astype(BF16), v).astype(o_ref.dtype)


def _mem_attention(qkv, kv, b, s, tq):
    width = MEM_HEADS * MEM_HEAD_DIM
    mlen = kv.shape[1]
    return pl.pallas_call(
        _mem_attn_kernel, grid=(b, s // tq),
        in_specs=[pl.BlockSpec((1, tq, width), lambda bi, qi: (bi, qi, 3)),
                  pl.BlockSpec((1, mlen, 2 * width), lambda bi, qi: (bi, 0, 0))],
        out_specs=pl.BlockSpec((1, tq, width), lambda bi, qi: (bi, qi, 0)),
        out_shape=jax.ShapeDtypeStruct((b, s, width), BF16),
        compiler_params=pltpu.CompilerParams(
            dimension_semantics=("arbitrary", "arbitrary"), vmem_limit_bytes=VMEM_LIMIT),
        name="mem_attention",
    )(qkv, kv)


def _merge_mlp_kernel(x_ref, ysb_ref, yssd_ref, ymem_ref, g0_ref, g1_ref, g2_ref,
                      wsb_ref, wssd_ref, wmem_ref, wo_ref, gmix_ref,
                      gpre_ref, wup_ref, wdown_ref, gpost_ref, o_ref, *, chunk):
    merged = (_sigmoid(g0_ref[...]) * _dot(ysb_ref[...], wsb_ref[...])
              + _sigmoid(g1_ref[...]) * _dot(yssd_ref[...], wssd_ref[...])
              + _sigmoid(g2_ref[...]) * _dot(ymem_ref[...], wmem_ref[...]))
    mix = _dot(merged.astype(BF16), wo_ref[...])
    h = x_ref[...] + _rms_rows(mix, gmix_ref[...])

    u = _rms_rows(h, gpre_ref[...]).astype(BF16)
    ff = None
    for c in range(wup_ref.shape[1] // chunk):
        hid = _dot(u, wup_ref[:, c * chunk:(c + 1) * chunk])
        act = jnp.square(jnp.maximum(hid, 0.0)).astype(BF16)
        t = _dot(act, wdown_ref[c * chunk:(c + 1) * chunk, :])
        ff = t if ff is None else ff + t
    o_ref[...] = h + _rms_rows(ff, gpost_ref[...])


def _merge_mlp(x, y_sb, y_ssd, y_mem, proj, w_sb, w_ssd, w_mem, w_o, g_mix,
               g_pre, w_up, w_down, g_post, tm):
    m, d = x.shape
    inner = y_ssd.shape[1]
    gate0 = proj.shape[1] // d - N_GATES
    tile = lambda w: pl.BlockSpec((tm, w), lambda i: (i, 0))
    gate = lambda k: pl.BlockSpec((tm, d), lambda i: (i, gate0 + k))
    full = lambda a: pl.BlockSpec(a.shape, lambda i: (0, 0), pipeline_mode=pl.Buffered(1))
    return pl.pallas_call(
        functools.partial(_merge_mlp_kernel, chunk=1024), grid=(m // tm,),
        in_specs=[tile(d), tile(d), tile(inner), tile(d), gate(0), gate(1), gate(2),
                  full(w_sb), full(w_ssd), full(w_mem), full(w_o), full(g_mix),
                  full(g_pre), full(w_up), full(w_down), full(g_post)],
        out_specs=tile(d),
        out_shape=jax.ShapeDtypeStruct((m, d), F32),
        compiler_params=pltpu.CompilerParams(
            dimension_semantics=("arbitrary",), vmem_limit_bytes=VMEM_LIMIT),
        name="merge_mlp",
    )(x, y_sb, y_ssd, y_mem, proj, proj, proj, w_sb, w_ssd, w_mem, w_o, g_mix,
      g_pre, w_up, w_down, g_post)


def _layer(h, mem, norm_mix_pre, w_in, conv_w, conv_b, dt_bias, a_log, d_skip, ssd_norm,
           norm_mem, w_mem_kv, w_sb_out, w_ssd_out, w_mem_out, w_o, norm_mix_post,
           norm_mlp_pre, w_up, w_down, norm_mlp_post):
    b, s, d = h.shape
    m = b * s
    row = lambda v: v.reshape(1, -1)
    x2 = h.reshape(m, d)

    sb_w = 3 * SB_HEADS * SB_HEAD_DIM
    inner = SSD_HEADS * SSD_HEAD_DIM
    conv_dim = inner + 2 * SSD_GROUPS * SSD_STATE
    o_z, o_xbc, o_dt = sb_w, sb_w + inner, sb_w + inner + conv_dim
    o_memq = o_dt + SSD_HEADS
    o_gate = o_memq + MEM_HEADS * MEM_HEAD_DIM

    tn = 1024
    w_t = jnp.swapaxes(w_in, 0, 1).astype(BF16)
    w_dt_t = jnp.pad(w_t[o_dt:o_memq], ((0, LANES - SSD_HEADS), (0, 0)))
    kinds = (["a"] * (sb_w // tn) + ["b"] * (inner // tn) + ["c"] * (conv_dim // tn)
             + ["a"] * ((o_gate - o_memq) // tn) + ["b"] * (N_GATES * d // tn))
    qkv, proj, xbc, dt, dtt = _in_proj(x2, row(norm_mix_pre), w_t, (o_dt, o_memq), kinds, w_dt_t,
                                       SSD_HEADS, 1024, tn)
    qkv = qkv.reshape(b, s, -1)

    y_sb = _sb_attention(qkv, b, s, 1024)
    y_ssd = _ssd_branch(proj.reshape(b, s, -1), xbc, dt, dtt, b, s,
                        conv_w, conv_b, dt_bias, a_log, d_skip, ssd_norm)

    mlen = mem.shape[1]
    kv = _norm_proj(mem.reshape(b * mlen, d), row(norm_mem), w_mem_kv.astype(BF16), BF16, b * mlen, 1024)
    y_mem = _mem_attention(qkv, kv.reshape(b, mlen, -1), b, s, 512)

    out = _merge_mlp(x2, y_sb.reshape(m, -1), y_ssd.reshape(m, -1), y_mem.reshape(m, -1), proj,
                     w_sb_out.astype(BF16), w_ssd_out.astype(BF16), w_mem_out.astype(BF16),
                     w_o.astype(BF16), row(norm_mix_post),
                     row(norm_mlp_pre), w_up.astype(BF16), w_down.astype(BF16), row(norm_mlp_post), 256)
    return out.reshape(b, s, d)


def kernel(x, mem, norm_mix_pre, w_in, conv_w, conv_b, dt_bias, a_log, d_skip, ssd_norm, norm_mem, w_mem_kv, w_sb_out, w_ssd_out, w_mem_out, w_o, norm_mix_post, norm_mlp_pre, w_up, w_down, norm_mlp_post):
    h = x
    for layer in range(w_in.shape[0]):
        h = _layer(h, mem, norm_mix_pre[layer], w_in[layer], conv_w[layer], conv_b[layer],
                   dt_bias[layer], a_log[layer], d_skip[layer], ssd_norm[layer], norm_mem[layer],
                   w_mem_kv[layer], w_sb_out[layer], w_ssd_out[layer], w_mem_out[layer], w_o[layer],
                   norm_mix_post[layer], norm_mlp_pre[layer], w_up[layer], w_down[layer],
                   norm_mlp_post[layer])
    return h
```

```python
import functools

import jax
import jax.numpy as jnp
import numpy as np
from jax import lax
from jax.experimental import pallas as pl
from jax.experimental.pallas import tpu as pltpu

F32 = jnp.float32
BF16 = jnp.bfloat16

EPS = 1e-6
LOG2E = 1.4426950408889634
LANES = 128
SB_HEADS = 16
SB_HEAD_DIM = 64
SB_BLOCK = 128
SSD_HEADS = 32
SSD_HEAD_DIM = 64
SSD_GROUPS = 4
SSD_STATE = 128
SSD_CHUNK = 128
SSD_CONV = 4
MEM_HEADS = 4
MEM_HEAD_DIM = 256
N_GATES = 3
VMEM_LIMIT = 56 * 1024 * 1024

SB_EXP_UNDERFLOW = 110.0
SB_NO_KEYS = 1e30
SB_FIRST_PASS = 3


SSD_STRIDE = 4


def _ssd_row_time(pos):
    span = 8 * SSD_STRIDE
    return (pos & -span) + (pos & 7) * SSD_STRIDE + ((pos >> 3) & (SSD_STRIDE - 1))


def _dot(a, b):
    return jnp.dot(a, b, preferred_element_type=F32)


def _dot_nt(a, b):
    return lax.dot_general(a, b, (((1,), (1,)), ((), ())), preferred_element_type=F32)


def _split_bf16(x, parts):
    out = []
    rem = x
    for _ in range(parts):
        p = rem.astype(BF16)
        out.append(p)
        rem = rem - p.astype(F32)
    return out


def _dot_split_lhs(x, m, parts):
    acc = None
    for p in _split_bf16(x, parts):
        t = _dot(p, m)
        acc = t if acc is None else acc + t
    return acc


def _dot_split_rhs(m, x, parts):
    acc = None
    for p in _split_bf16(x, parts):
        t = _dot(m, p)
        acc = t if acc is None else acc + t
    return acc


def _rms_rows(x, gain):
    ms = jnp.mean(x * x, axis=-1, keepdims=True)
    return x * lax.rsqrt(ms + EPS) * gain


def _softplus(x):
    return jnp.maximum(x, 0.0) + jnp.log1p(jnp.exp(-jnp.abs(x)))


def _sigmoid(x):
    return 1.0 / (1.0 + jnp.exp2(x * -LOG2E))


def _silu(x):
    h = 0.5 * x
    return h + h * jnp.tanh(h)


def _norm_proj_kernel(x_ref, g_ref, w_ref, o_ref, u_ref):
    @pl.when(pl.program_id(1) == 0)
    def _():
        u_ref[...] = _rms_rows(x_ref[...], g_ref[...]).astype(BF16)

    o_ref[...] = _dot(u_ref[...], w_ref[...]).astype(o_ref.dtype)


def _norm_proj(x, gain, w, out_dtype, tm, tn):
    m, d = x.shape
    n = w.shape[1]
    return pl.pallas_call(
        _norm_proj_kernel, grid=(m // tm, n // tn),
        in_specs=[pl.BlockSpec((tm, d), lambda i, j: (i, 0)),
                  pl.BlockSpec((1, d), lambda i, j: (0, 0)),
                  pl.BlockSpec((d, tn), lambda i, j: (0, j))],
        out_specs=pl.BlockSpec((tm, tn), lambda i, j: (i, j)),
        out_shape=jax.ShapeDtypeStruct((m, n), out_dtype),
        scratch_shapes=[pltpu.VMEM((tm, d), BF16)],
        compiler_params=pltpu.CompilerParams(dimension_semantics=("arbitrary", "arbitrary"),
                                             vmem_limit_bytes=VMEM_LIMIT),
        name="norm_proj",
    )(x, gain, w)


def _in_proj_kernel(x_ref, g_ref, wt_ref, wdtt_ref, oa_ref, oc_ref, dt_ref, dtt_ref, u_ref, *, a_tiles, nh):
    j = pl.program_id(1)

    @pl.when(j == 0)
    def _():
        u = _rms_rows(x_ref[...], g_ref[...]).astype(BF16)
        u_ref[...] = u
        dt_ref[...] = _dot_nt(u, wdtt_ref[...])
        dtt_ref[...] = _dot_nt(wdtt_ref[0:nh, :], u)

    is_a = functools.reduce(jnp.logical_or, [j == t for t in a_tiles])

    @pl.when(is_a)
    def _():
        oa_ref[...] = _dot_nt(u_ref[...], wt_ref[...]).astype(oa_ref.dtype)

    @pl.when(jnp.logical_not(is_a))
    def _():
        res = _dot_nt(u_ref[...], wt_ref[...])
        for r in range(oc_ref.shape[0]):
            for c in range(oc_ref.shape[1]):
                oc_ref[r, c] = res[r * LANES:(r + 1) * LANES, c * LANES:(c + 1) * LANES]


def _in_proj(x, gain, w_t, skip, a_tiles, w_dt_t, nh, tm, tn):
    m, d = x.shape
    nt = (w_t.shape[0] - (skip[1] - skip[0])) // tn
    c_tiles = [t for t in range(nt) if t not in a_tiles]
    skip_tile, skip_rows = skip[0] // tn, skip[1] - skip[0]

    def rank(tiles, j):
        return jnp.maximum(sum((j >= t).astype(jnp.int32) for t in tiles) - 1, 0)

    return pl.pallas_call(
        functools.partial(_in_proj_kernel, a_tiles=tuple(a_tiles), nh=nh), grid=(m // tm, nt),
        in_specs=[pl.BlockSpec((tm, d), lambda i, j: (i, 0), pipeline_mode=pl.Buffered(1)),
                  pl.BlockSpec((1, d), lambda i, j: (0, 0)),
                  pl.BlockSpec((pl.Element(tn), pl.Element(d)),
                               lambda i, j: ((j * (tn // skip_rows) + (j >= skip_tile).astype(jnp.int32)) * skip_rows, 0)),
                  pl.BlockSpec((LANES, d), lambda i, j: (0, 0))],
        out_specs=[pl.BlockSpec((tm, tn), lambda i, j: (i, rank(a_tiles, j))),
                   pl.BlockSpec((tm // LANES, tn // LANES, LANES, LANES), lambda i, j: (i, rank(c_tiles, j), 0, 0)),
                   pl.BlockSpec((tm, LANES), lambda i, j: (i, 0)),
                   pl.BlockSpec((nh, tm), lambda i, j: (0, i))],
        out_shape=[jax.ShapeDtypeStruct((m, len(a_tiles) * tn), BF16),
                   jax.ShapeDtypeStruct((m // LANES, len(c_tiles) * tn // LANES, LANES, LANES), F32),
                   jax.ShapeDtypeStruct((m, LANES), F32),
                   jax.ShapeDtypeStruct((nh, m), F32)],
        scratch_shapes=[pltpu.VMEM((tm, d), BF16)],
        compiler_params=pltpu.CompilerParams(dimension_semantics=("arbitrary", "arbitrary"),
                                             vmem_limit_bytes=VMEM_LIMIT),
        name="in_proj",
    )(x, gain, w_t, w_dt_t)


def _sb_kernel(q_ref, k_ref, v_ref, mm_ref, o_ref, qs_ref, kc_ref, vc_ref, acc_ref, c_ref, *, nsub):
    qi = pl.program_id(2)
    blk = SB_BLOCK
    nblk = v_ref.shape[1] // blk
    head0 = lax.broadcasted_iota(jnp.int32, (blk, LANES), 1) < SB_HEAD_DIM
    key = jnp.bitwise_and(lax.broadcasted_iota(jnp.int32, (blk, 2 * blk), 1), blk - 1)
    causal = key < lax.broadcasted_iota(jnp.int32, (blk, 2 * blk), 0)
    scale = SB_HEAD_DIM ** -0.5

    def stack_heads(x):
        x = x.astype(F32)
        return jnp.concatenate([jnp.where(head0, x, 0.0), jnp.where(head0, 0.0, x)], axis=0).astype(BF16)

    @pl.when(qi == 0)
    def _():
        def fill(j, carry):
            off = pl.multiple_of(j * blk, blk)
            kc_ref[j] = stack_heads(k_ref[0, pl.ds(off, blk), :])
            vc_ref[j] = stack_heads(v_ref[0, pl.ds(off, blk), :])
            return carry
        lax.fori_loop(0, nblk, fill, 0)

    qs_ref[...] = (q_ref[0].astype(F32) * scale).astype(BF16)

    def visit(s, count, first):
        units = [(t, d) for t in range(nsub) for d in range(count)]
        diag = {u: first and u[1] == 0 for u in units}
        js = {(t, d): qi * nsub + t - (s + d) for t, d in units}
        jcs = {u: js[u] if diag[u] else jnp.maximum(js[u], 0) for u in units}
        zs = {(t, d): _dot_nt(qs_ref[t * blk:(t + 1) * blk, :], kc_ref[jcs[t, d]]) for t, d in units}
        log_betas, sums = {}, {}
        for u in units:
            z = zs[u]
            sp = jnp.log(1.0 + jnp.exp2(jnp.abs(z) * -LOG2E))
            log_beta = jnp.minimum(z, 0.0) - sp
            log_keep = log_beta - z
            if diag[u]:
                log_keep = jnp.where(causal, log_keep, 0.0)
            log_betas[u] = log_beta
            hi, lo = _split_bf16(log_keep, 2)
            sums[u] = [_dot(jnp.concatenate([hi[:, h * blk:(h + 1) * blk], lo[:, h * blk:(h + 1) * blk]], axis=1),
                            mm_ref[...]) for h in range(2)]
        cmax = None
        for t in range(nsub):
            c = None if first else c_ref[t]
            acc = None if first else acc_ref[t]
            for d in range(count):
                u = (t, d)
                r0, r1 = sums[u]
                later = jnp.concatenate([r0[:, :blk], r1[:, :blk]], axis=1)
                total = jnp.concatenate([r0[:, blk:], r1[:, blk:]], axis=1)
                if diag[u]:
                    w = jnp.where(causal, jnp.exp(log_betas[u] + later), 0.0)
                    c = total
                else:
                    c = jnp.where(js[u] >= 0, c, -SB_NO_KEYS)
                    w = jnp.exp(log_betas[u] + later + c)
                    c = c + total
                pv = _dot(w.astype(BF16), vc_ref[jcs[u]])
                acc = pv if acc is None else acc + pv
            acc_ref[t] = acc
            c_ref[t] = c
            cmax = c if cmax is None else jnp.maximum(cmax, c)
        return jnp.max(cmax)

    last = qi * nsub + nsub - 1

    def cond(st):
        s, cm = st
        return jnp.logical_and(s <= last, cm > -SB_EXP_UNDERFLOW)

    def body(st):
        return st[0] + 1, visit(st[0], 1, False)

    lax.while_loop(cond, body, (jnp.int32(SB_FIRST_PASS), visit(0, SB_FIRST_PASS, True)))
    for t in range(nsub):
        o_ref[0, t * blk:(t + 1) * blk, :] = acc_ref[t].astype(o_ref.dtype)


def _sb_attention(qkv, b, s, tq):
    pairs = SB_HEADS * SB_HEAD_DIM // LANES
    nsub = tq // SB_BLOCK
    idx = np.arange(SB_BLOCK)
    later = (idx[:, None] > idx[None, :]).astype(np.float32)
    m = np.concatenate([later, np.ones((SB_BLOCK, SB_BLOCK), np.float32)], axis=1)
    mm = jnp.asarray(np.concatenate([m, m], axis=0), BF16)
    return pl.pallas_call(
        functools.partial(_sb_kernel, nsub=nsub),
        grid=(b, pairs, s // tq),
        in_specs=[
            pl.BlockSpec((1, tq, LANES), lambda bi, hp, qi: (bi, qi, hp)),
            pl.BlockSpec((1, s, LANES), lambda bi, hp, qi: (bi, 0, pairs + hp)),
            pl.BlockSpec((1, s, LANES), lambda bi, hp, qi: (bi, 0, 2 * pairs + hp)),
            pl.BlockSpec((2 * SB_BLOCK, 2 * SB_BLOCK), lambda bi, hp, qi: (0, 0)),
        ],
        out_specs=pl.BlockSpec((1, tq, LANES), lambda bi, hp, qi: (bi, qi, hp)),
        out_shape=jax.ShapeDtypeStruct((b, s, SB_HEADS * SB_HEAD_DIM), BF16),
        scratch_shapes=[pltpu.VMEM((tq, LANES), BF16),
                        pltpu.VMEM((s // SB_BLOCK, 2 * SB_BLOCK, LANES), BF16),
                        pltpu.VMEM((s // SB_BLOCK, 2 * SB_BLOCK, LANES), BF16),
                        pltpu.VMEM((nsub, SB_BLOCK, LANES), F32),
                        pltpu.VMEM((nsub, SB_BLOCK, 2 * SB_BLOCK), F32)],
        compiler_params=pltpu.CompilerParams(
            dimension_semantics=("arbitrary", "arbitrary", "arbitrary"), vmem_limit_bytes=VMEM_LIMIT),
        name="sb_attention",
    )(qkv, qkv, qkv, mm)


def _ssd_kernel(z_ref, xs_ref, bc_ref, dt_ref, dtt_ref, cw_ref, cb_ref,
                dtb_ref, dtbt_ref, alog_ref, alogt_ref, dexp_ref, gn_ref,
                tri_ref, trit_ref, e_ref,
                o_ref, prev_ref, state_ref, ydiag_ref, yout_ref):
    ck = SSD_CHUNK
    st = SSD_STRIDE
    span = 8 * st
    inner = SSD_HEADS * SSD_HEAD_DIM
    gw = inner // SSD_GROUPS
    gs = SSD_GROUPS * SSD_STATE
    tile_rows = [pl.ds(g * span + i, 8, stride=st) for g in range(ck // span) for i in range(st)]

    @pl.when(pl.program_id(1) == 0)
    def _():
        prev_ref[...] = jnp.zeros_like(prev_ref)
        state_ref[...] = jnp.zeros_like(state_ref)

    first_row = lax.broadcasted_iota(jnp.int32, (8, LANES), 0) == 0

    def conv_silu(c):
        lanes = slice(c * LANES, (c + 1) * LANES)
        nx = xs_ref.shape[1]
        src_ref, sc = (xs_ref, c) if c < nx else (bc_ref, c - nx)
        tiles = [src_ref[0, sc, rows, :] for rows in tile_rows]
        down = {-j: pltpu.roll(prev_ref[j - 1, :, lanes], 1, 0) for j in range(1, SSD_CONV)}
        for n, tile in enumerate(tiles):
            if n % st >= st - (SSD_CONV - 1):
                down[n] = pltpu.roll(tile, 1, 0)
        wrapped = {}

        def back(n, k):
            if n % st >= k:
                return tiles[n - k]
            if n - k not in wrapped:
                wrapped[n - k] = jnp.where(first_row, down[n - k], down[n - k + st])
            return wrapped[n - k]

        out = []
        for n in range(len(tiles)):
            acc = cb_ref[:, lanes]
            for k in range(SSD_CONV):
                acc = acc + back(n, k) * cw_ref[SSD_CONV - 1 - k:SSD_CONV - k, lanes]
            out.append(_silu(acc))
        for j in range(1, SSD_CONV):
            prev_ref[j - 1, :, lanes] = tiles[len(tiles) - j]
        return jnp.concatenate(out, axis=0)

    xs = jnp.concatenate([conv_silu(c) for c in range(inner // LANES)], axis=1)
    bcm = jnp.concatenate([conv_silu(c) for c in range(inner // LANES, (inner + 2 * gs) // LANES)], axis=1)

    dt_raw = jnp.concatenate([dt_ref[rows, :] for rows in tile_rows], axis=0)
    dt = _softplus(dt_raw + dtb_ref[...])
    dtt = _softplus(dtt_ref[...] + dtbt_ref[...])
    da = dt * (-jnp.exp(alog_ref[...]))
    dat = dtt * (-jnp.exp(alogt_ref[...]))
    a_cs = _dot_split_rhs(tri_ref[...], da, 3)
    a_cst = _dot_split_lhs(dat, trit_ref[...], 3)
    a_last = a_cs[ck - 1:ck, :]

    e = e_ref[...]
    dt_x = _dot_split_lhs(dt, e, 2)
    ea_x = _dot_split_lhs(jnp.exp(a_cs), e, 2)
    dte_x = _dot_split_lhs(jnp.exp(a_last - a_cs), e, 2)

    x_dt = xs * dt_x
    x_dt16 = x_dt.astype(BF16)
    x_end16 = (x_dt * dte_x).astype(BF16)

    lane = lax.broadcasted_iota(jnp.int32, (ck, LANES), 1)
    row = lax.broadcasted_iota(jnp.int32, (ck, LANES), 0)
    head0 = lane < SSD_HEAD_DIM
    tri_mask = _ssd_row_time(lane) <= _ssd_row_time(row)

    heads_per_group = SSD_HEADS // SSD_GROUPS
    for g in range(SSD_GROUPS):
        bg = bcm[:, g * SSD_STATE:(g + 1) * SSD_STATE]
        cg16 = bcm[:, gs + g * SSD_STATE:gs + (g + 1) * SSD_STATE].astype(BF16)
        cb = _dot_nt(cg16, bg.astype(BF16))
        for rp in range(heads_per_group // 2):
            pair = g * (heads_per_group // 2) + rp
            xp = x_dt16[:, pair * LANES:(pair + 1) * LANES]
            ys = []
            for hh in range(2):
                r = 2 * pair + hh
                seg = a_cs[:, r:r + 1] - a_cst[r:r + 1, :]
                lmat = cb * jnp.exp(jnp.where(tri_mask, seg, -jnp.inf))
                ys.append(_dot(lmat.astype(BF16), xp))
            ydiag_ref[:, pair * LANES:(pair + 1) * LANES] = jnp.where(head0, ys[0], ys[1])

        cols = slice(g * gw, (g + 1) * gw)
        st = state_ref[:, cols]
        y_off = _dot(cg16, st.astype(BF16)) * ea_x[:, cols]
        contrib = _dot(bg.T.astype(BF16), x_end16[:, cols])
        state_ref[:, cols] = st * ea_x[ck - 1:ck, cols] + contrib

        y = ydiag_ref[:, cols] + y_off + dexp_ref[:, cols] * xs[:, cols]
        chunks = range(g * gw // LANES, (g + 1) * gw // LANES)
        for c in chunks:
            for n, rows in enumerate(tile_rows):
                yout_ref[c, rows, :] = y[8 * n:8 * n + 8, (c - chunks[0]) * LANES:(c - chunks[0] + 1) * LANES]
        y = jnp.concatenate([yout_ref[c] for c in chunks], axis=1)
        zg = jnp.concatenate([z_ref[0, c] for c in chunks], axis=1)
        y = y * _silu(zg)
        ms = jnp.mean(y * y, axis=-1, keepdims=True)
        o_ref[0, :, cols] = (y * lax.rsqrt(ms + EPS) * gn_ref[:, cols]).astype(o_ref.dtype)


def _ssd_branch(proj, dt, dtt, b, s, conv_w, conv_b, dt_bias, a_log, d_skip, ssd_norm):
    inner = SSD_HEADS * SSD_HEAD_DIM
    gs = SSD_GROUPS * SSD_STATE
    nc = s // SSD_CHUNK
    nx, nbc = inner // LANES, 2 * gs // LANES
    chunked = lambda n, at: pl.BlockSpec((1, n, SSD_CHUNK, LANES), lambda bi, ci: (bi * nc + ci, at // n, 0, 0))
    idx = np.arange(SSD_CHUNK)
    when = _ssd_row_time(idx)
    tri = jnp.asarray((when[None, :] <= when[:, None]).astype(np.float32), BF16)
    trit = jnp.asarray((idx[:, None] <= when[None, :]).astype(np.float32), BF16)
    expand = np.zeros((LANES, inner), np.float32)
    expand[np.arange(inner) // SSD_HEAD_DIM, np.arange(inner)] = 1.0
    expand = jnp.asarray(expand, BF16)

    pad = LANES - SSD_HEADS
    row = lambda v: v.reshape(1, -1)
    dtb = jnp.pad(row(dt_bias), ((0, 0), (0, pad)))
    alog = jnp.pad(row(a_log), ((0, 0), (0, pad)))
    dexp = row(jnp.repeat(d_skip, SSD_HEAD_DIM))

    const = lambda shape: pl.BlockSpec(shape, lambda bi, ci: (0,) * len(shape))
    return pl.pallas_call(
        _ssd_kernel, grid=(b, nc),
        in_specs=[
            chunked(nx, 0),
            chunked(nx, nx),
            chunked(nbc, 2 * nx),
            pl.BlockSpec((SSD_CHUNK, LANES), lambda bi, ci: (bi * nc + ci, 0)),
            pl.BlockSpec((SSD_HEADS, SSD_CHUNK), lambda bi, ci: (0, bi * nc + ci)),
            const((SSD_CONV, inner + 2 * gs)), const((1, inner + 2 * gs)),
            const((1, LANES)), const((SSD_HEADS, 1)), const((1, LANES)), const((SSD_HEADS, 1)),
            const((1, inner)), const((1, inner)),
            const((SSD_CHUNK, SSD_CHUNK)), const((SSD_CHUNK, SSD_CHUNK)), const((LANES, inner)),
        ],
        out_specs=pl.BlockSpec((1, SSD_CHUNK, inner), lambda bi, ci: (bi, ci, 0)),
        out_shape=jax.ShapeDtypeStruct((b, s, inner), BF16),
        scratch_shapes=[pltpu.VMEM((SSD_CONV - 1, 8, inner + 2 * gs), F32),
                        pltpu.VMEM((SSD_STATE, inner), F32),
                        pltpu.VMEM((SSD_CHUNK, inner), F32),
                        pltpu.VMEM((inner // LANES, SSD_CHUNK, LANES), F32)],
        compiler_params=pltpu.CompilerParams(
            dimension_semantics=("arbitrary", "arbitrary"), vmem_limit_bytes=VMEM_LIMIT),
        name="ssd_branch",
    )(proj, proj, proj, dt, dtt, conv_w, row(conv_b),
      dtb, dt_bias.reshape(-1, 1), alog, a_log.reshape(-1, 1), dexp, row(ssd_norm),
      tri, trit, expand)


def _mem_attn_kernel(q_ref, kv_ref, o_ref):
    width = MEM_HEADS * MEM_HEAD_DIM
    scale = MEM_HEAD_DIM ** -0.5
    for h in range(MEM_HEADS):
        cols = slice(h * MEM_HEAD_DIM, (h + 1) * MEM_HEAD_DIM)
        q = q_ref[0, :, cols]
        k = kv_ref[0, :, cols]
        v = kv_ref[0, :, width + h * MEM_HEAD_DIM:width + (h + 1) * MEM_HEAD_DIM]
        sc = _dot_nt(q, k) * scale
        p = jnp.exp(sc - jnp.max(sc, axis=-1, keepdims=True))
        probs = p / jnp.sum(p, axis=-1, keepdims=True)
        o_ref[0, :, cols] = _dot(probs.astype(BF16), v).astype(o_ref.dtype)


def _mem_attention(qkv, kv, b, s, tq):
    width = MEM_HEADS * MEM_HEAD_DIM
    mlen = kv.shape[1]
    return pl.pallas_call(
        _mem_attn_kernel, grid=(b, s // tq),
        in_specs=[pl.BlockSpec((1, tq, width), lambda bi, qi: (bi, qi, 3)),
                  pl.BlockSpec((1, mlen, 2 * width), lambda bi, qi: (bi, 0, 0))],
        out_specs=pl.BlockSpec((1, tq, width), lambda bi, qi: (bi, qi, 0)),
        out_shape=jax.ShapeDtypeStruct((b, s, width), BF16),
        compiler_params=pltpu.CompilerParams(
            dimension_semantics=("arbitrary", "arbitrary"), vmem_limit_bytes=VMEM_LIMIT),
        name="mem_attention",
    )(qkv, kv)


def _merge_mlp_kernel(x_ref, ysb_ref, yssd_ref, ymem_ref, g0_ref, g1_ref, g2_ref,
                      wsb_ref, wssd_ref, wmem_ref, wo_ref, gmix_ref,
                      gpre_ref, wup_ref, wdown_ref, gpost_ref, o_ref, *, chunk):
    def gate(g_ref):
        logits = jnp.concatenate([jnp.concatenate([g_ref[r, c] for c in range(g_ref.shape[1])], axis=1)
                                  for r in range(g_ref.shape[0])], axis=0)
        return _sigmoid(logits)

    merged = (gate(g0_ref) * _dot(ysb_ref[...], wsb_ref[...])
              + gate(g1_ref) * _dot(yssd_ref[...], wssd_ref[...])
              + gate(g2_ref) * _dot(ymem_ref[...], wmem_ref[...]))
    mix = _dot(merged.astype(BF16), wo_ref[...])
    h = x_ref[...] + _rms_rows(mix, gmix_ref[...])

    u = _rms_rows(h, gpre_ref[...]).astype(BF16)
    ff = None
    for c in range(wup_ref.shape[1] // chunk):
        hid = _dot(u, wup_ref[:, c * chunk:(c + 1) * chunk])
        act = jnp.square(jnp.maximum(hid, 0.0)).astype(BF16)
        t = _dot(act, wdown_ref[c * chunk:(c + 1) * chunk, :])
        ff = t if ff is None else ff + t
    o_ref[...] = h + _rms_rows(ff, gpost_ref[...])


def _merge_mlp(x, y_sb, y_ssd, y_mem, proj, w_sb, w_ssd, w_mem, w_o, g_mix,
               g_pre, w_up, w_down, g_post, tm):
    m, d = x.shape
    inner = y_ssd.shape[1]
    dc = d // LANES
    gate0 = proj.shape[1] // dc - N_GATES
    tile = lambda w: pl.BlockSpec((tm, w), lambda i: (i, 0))
    gate = lambda k: pl.BlockSpec((tm // LANES, dc, LANES, LANES), lambda i: (i, gate0 + k, 0, 0))
    full = lambda a: pl.BlockSpec(a.shape, lambda i: (0, 0), pipeline_mode=pl.Buffered(1))
    return pl.pallas_call(
        functools.partial(_merge_mlp_kernel, chunk=1024), grid=(m // tm,),
        in_specs=[tile(d), tile(d), tile(inner), tile(d), gate(0), gate(1), gate(2),
                  full(w_sb), full(w_ssd), full(w_mem), full(w_o), full(g_mix),
                  full(g_pre), full(w_up), full(w_down), full(g_post)],
        out_specs=tile(d),
        out_shape=jax.ShapeDtypeStruct((m, d), F32),
        compiler_params=pltpu.CompilerParams(
            dimension_semantics=("arbitrary",), vmem_limit_bytes=VMEM_LIMIT),
        name="merge_mlp",
    )(x, y_sb, y_ssd, y_mem, proj, proj, proj, w_sb, w_ssd, w_mem, w_o, g_mix,
      g_pre, w_up, w_down, g_post)


def _layer(h, mem, norm_mix_pre, w_in, conv_w, conv_b, dt_bias, a_log, d_skip, ssd_norm,
           norm_mem, w_mem_kv, w_sb_out, w_ssd_out, w_mem_out, w_o, norm_mix_post,
           norm_mlp_pre, w_up, w_down, norm_mlp_post):
    b, s, d = h.shape
    m = b * s
    row = lambda v: v.reshape(1, -1)
    x2 = h.reshape(m, d)

    sb_w = 3 * SB_HEADS * SB_HEAD_DIM
    inner = SSD_HEADS * SSD_HEAD_DIM
    conv_dim = inner + 2 * SSD_GROUPS * SSD_STATE
    o_z, o_xbc, o_dt = sb_w, sb_w + inner, sb_w + inner + conv_dim
    o_memq = o_dt + SSD_HEADS
    o_gate = o_memq + MEM_HEADS * MEM_HEAD_DIM

    tn = 1024
    w_t = jnp.swapaxes(w_in, 0, 1).astype(BF16)
    w_dt_t = jnp.pad(w_t[o_dt:o_memq], ((0, LANES - SSD_HEADS), (0, 0)))
    a_tiles = list(range(sb_w // tn)) + list(range(o_dt // tn, (o_dt + o_gate - o_memq) // tn))
    qkv, proj, dt, dtt = _in_proj(x2, row(norm_mix_pre), w_t, (o_dt, o_memq), a_tiles, w_dt_t,
                                  SSD_HEADS, 2048, tn)
    qkv = qkv.reshape(b, s, -1)

    y_sb = _sb_attention(qkv, b, s, 1024)
    y_ssd = _ssd_branch(proj, dt, dtt, b, s, conv_w, conv_b, dt_bias, a_log, d_skip, ssd_norm)

    mlen = mem.shape[1]
    kv = _norm_proj(mem.reshape(b * mlen, d), row(norm_mem), w_mem_kv.astype(BF16), BF16, b * mlen, 1024)
    y_mem = _mem_attention(qkv, kv.reshape(b, mlen, -1), b, s, 512)

    out = _merge_mlp(x2, y_sb.reshape(m, -1), y_ssd.reshape(m, -1), y_mem.reshape(m, -1), proj,
                     w_sb_out.astype(BF16), w_ssd_out.astype(BF16), w_mem_out.astype(BF16),
                     w_o.astype(BF16), row(norm_mix_post),
                     row(norm_mlp_pre), w_up.astype(BF16), w_down.astype(BF16), row(norm_mlp_post), 256)
    return out.reshape(b, s, d)


def kernel(x, mem, norm_mix_pre, w_in, conv_w, conv_b, dt_bias, a_log, d_skip, ssd_norm, norm_mem, w_mem_kv, w_sb_out, w_ssd_out, w_mem_out, w_o, norm_mix_post, norm_mlp_pre, w_up, w_down, norm_mlp_post):
    h = x
    for layer in range(w_in.shape[0]):
        h = _layer(h, mem, norm_mix_pre[layer], w_in[layer], conv_w[layer], conv_b[layer],
                   dt_bias[layer], a_log[layer], d_skip[layer], ssd_norm[layer], norm_mem[layer],
                   w_mem_kv[layer], w_sb_out[layer], w_ssd_out[layer], w_mem_out[layer], w_o[layer],
                   norm_mix_post[layer], norm_mlp_pre[layer], w_up[layer], w_down[layer],
                   norm_mlp_post[layer])
    return h
```

```python
import functools

import jax
import jax.numpy as jnp
import numpy as np
from jax import lax
from jax.experimental import pallas as pl
from jax.experimental.pallas import tpu as pltpu

F32 = jnp.float32
BF16 = jnp.bfloat16

EPS = 1e-6
LOG2E = 1.4426950408889634
LANES = 128
SB_HEADS = 16
SB_HEAD_DIM = 64
SB_BLOCK = 128
SSD_HEADS = 32
SSD_HEAD_DIM = 64
SSD_GROUPS = 4
SSD_STATE = 128
SSD_CHUNK = 128
SSD_CONV = 4
MEM_HEADS = 4
MEM_HEAD_DIM = 256
N_GATES = 3
VMEM_LIMIT = 56 * 1024 * 1024

SB_EXP_UNDERFLOW = 110.0
SB_NO_KEYS = 1e30
SB_FIRST_PASS = 3


SSD_STRIDE = 4
SSD_CHUNKS_PER_STEP = 2


def _ssd_row_time(pos):
    span = 8 * SSD_STRIDE
    return (pos & -span) + (pos & 7) * SSD_STRIDE + ((pos >> 3) & (SSD_STRIDE - 1))


def _dot(a, b):
    return jnp.dot(a, b, preferred_element_type=F32)


def _dot_nt(a, b):
    return lax.dot_general(a, b, (((1,), (1,)), ((), ())), preferred_element_type=F32)


def _split_bf16(x, parts):
    out = []
    rem = x
    for _ in range(parts):
        p = rem.astype(BF16)
        out.append(p)
        rem = rem - p.astype(F32)
    return out


def _dot_split_lhs(x, m, parts):
    acc = None
    for p in _split_bf16(x, parts):
        t = _dot(p, m)
        acc = t if acc is None else acc + t
    return acc


def _dot_split_rhs(m, x, parts):
    acc = None
    for p in _split_bf16(x, parts):
        t = _dot(m, p)
        acc = t if acc is None else acc + t
    return acc


def _rms_rows(x, gain):
    ms = jnp.mean(x * x, axis=-1, keepdims=True)
    return x * lax.rsqrt(ms + EPS) * gain


def _softplus(x):
    return jnp.maximum(x, 0.0) + jnp.log1p(jnp.exp(-jnp.abs(x)))


def _sigmoid(x):
    return 1.0 / (1.0 + jnp.exp2(x * -LOG2E))


def _silu(x):
    h = 0.5 * x
    return h + h * jnp.tanh(h)


def _norm_proj_kernel(x_ref, g_ref, w_ref, o_ref, u_ref):
    @pl.when(pl.program_id(1) == 0)
    def _():
        u_ref[...] = _rms_rows(x_ref[...], g_ref[...]).astype(BF16)

    o_ref[...] = _dot(u_ref[...], w_ref[...]).astype(o_ref.dtype)


def _norm_proj(x, gain, w, out_dtype, tm, tn):
    m, d = x.shape
    n = w.shape[1]
    return pl.pallas_call(
        _norm_proj_kernel, grid=(m // tm, n // tn),
        in_specs=[pl.BlockSpec((tm, d), lambda i, j: (i, 0)),
                  pl.BlockSpec((1, d), lambda i, j: (0, 0)),
                  pl.BlockSpec((d, tn), lambda i, j: (0, j))],
        out_specs=pl.BlockSpec((tm, tn), lambda i, j: (i, j)),
        out_shape=jax.ShapeDtypeStruct((m, n), out_dtype),
        scratch_shapes=[pltpu.VMEM((tm, d), BF16)],
        compiler_params=pltpu.CompilerParams(dimension_semantics=("arbitrary", "arbitrary"),
                                             vmem_limit_bytes=VMEM_LIMIT),
        name="norm_proj",
    )(x, gain, w)


def _in_proj_kernel(x_ref, g_ref, wt_ref, wdtt_ref, oa_ref, oc_ref, dt_ref, dtt_ref, u_ref, *,
                    a_tiles, silu_tiles, nh):
    j = pl.program_id(1)

    @pl.when(j == 0)
    def _():
        u = _rms_rows(x_ref[...], g_ref[...]).astype(BF16)
        u_ref[...] = u
        dt_ref[...] = _dot_nt(u, wdtt_ref[...])
        dtt_ref[...] = _dot_nt(wdtt_ref[0:nh, :], u)

    tile_in = lambda tiles: functools.reduce(jnp.logical_or, [j == t for t in tiles])

    def store_chunked(res):
        for r in range(oc_ref.shape[0]):
            for c in range(oc_ref.shape[1]):
                oc_ref[r, c] = res[r * LANES:(r + 1) * LANES, c * LANES:(c + 1) * LANES]

    @pl.when(tile_in(a_tiles))
    def _():
        oa_ref[...] = _dot_nt(u_ref[...], wt_ref[...]).astype(oa_ref.dtype)

    @pl.when(tile_in(silu_tiles))
    def _():
        store_chunked(_silu(_dot_nt(u_ref[...], wt_ref[...])))

    @pl.when(jnp.logical_not(tile_in(tuple(a_tiles) + tuple(silu_tiles))))
    def _():
        store_chunked(_dot_nt(u_ref[...], wt_ref[...]))


def _in_proj(x, gain, w_t, skip, a_tiles, silu_tiles, w_dt_t, nh, tm, tn):
    m, d = x.shape
    nt = (w_t.shape[0] - (skip[1] - skip[0])) // tn
    c_tiles = [t for t in range(nt) if t not in a_tiles]
    skip_tile, skip_rows = skip[0] // tn, skip[1] - skip[0]

    def rank(tiles, j):
        return jnp.maximum(sum((j >= t).astype(jnp.int32) for t in tiles) - 1, 0)

    return pl.pallas_call(
        functools.partial(_in_proj_kernel, a_tiles=tuple(a_tiles), silu_tiles=tuple(silu_tiles), nh=nh),
        grid=(m // tm, nt),
        in_specs=[pl.BlockSpec((tm, d), lambda i, j: (i, 0), pipeline_mode=pl.Buffered(1)),
                  pl.BlockSpec((1, d), lambda i, j: (0, 0)),
                  pl.BlockSpec((pl.Element(tn), pl.Element(d)),
                               lambda i, j: ((j * (tn // skip_rows) + (j >= skip_tile).astype(jnp.int32)) * skip_rows, 0)),
                  pl.BlockSpec((LANES, d), lambda i, j: (0, 0))],
        out_specs=[pl.BlockSpec((tm, tn), lambda i, j: (i, rank(a_tiles, j))),
                   pl.BlockSpec((tm // LANES, tn // LANES, LANES, LANES), lambda i, j: (i, rank(c_tiles, j), 0, 0)),
                   pl.BlockSpec((tm, LANES), lambda i, j: (i, 0)),
                   pl.BlockSpec((nh, tm), lambda i, j: (0, i))],
        out_shape=[jax.ShapeDtypeStruct((m, len(a_tiles) * tn), BF16),
                   jax.ShapeDtypeStruct((m // LANES, len(c_tiles) * tn // LANES, LANES, LANES), F32),
                   jax.ShapeDtypeStruct((m, LANES), F32),
                   jax.ShapeDtypeStruct((nh, m), F32)],
        scratch_shapes=[pltpu.VMEM((tm, d), BF16)],
        compiler_params=pltpu.CompilerParams(dimension_semantics=("arbitrary", "arbitrary"),
                                             vmem_limit_bytes=VMEM_LIMIT),
        name="in_proj",
    )(x, gain, w_t, w_dt_t)


def _sb_kernel(q_ref, k_ref, v_ref, mm_ref, o_ref, qs_ref, kc_ref, vc_ref, acc_ref, c_ref, *, nsub):
    qi = pl.program_id(2)
    blk = SB_BLOCK
    nblk = v_ref.shape[1] // blk
    head0 = lax.broadcasted_iota(jnp.int32, (blk, LANES), 1) < SB_HEAD_DIM
    key = jnp.bitwise_and(lax.broadcasted_iota(jnp.int32, (blk, 2 * blk), 1), blk - 1)
    causal = key < lax.broadcasted_iota(jnp.int32, (blk, 2 * blk), 0)
    scale = SB_HEAD_DIM ** -0.5

    def stack_heads(x):
        x = x.astype(F32)
        return jnp.concatenate([jnp.where(head0, x, 0.0), jnp.where(head0, 0.0, x)], axis=0).astype(BF16)

    @pl.when(qi == 0)
    def _():
        def fill(j, carry):
            off = pl.multiple_of(j * blk, blk)
            kc_ref[j] = stack_heads(k_ref[0, pl.ds(off, blk), :])
            vc_ref[j] = stack_heads(v_ref[0, pl.ds(off, blk), :])
            return carry
        lax.fori_loop(0, nblk, fill, 0)

    qs_ref[...] = (q_ref[0].astype(F32) * scale).astype(BF16)

    def visit(s, count, first):
        units = [(t, d) for t in range(nsub) for d in range(count)]
        diag = {u: first and u[1] == 0 for u in units}
        js = {(t, d): qi * nsub + t - (s + d) for t, d in units}
        jcs = {u: js[u] if diag[u] else jnp.maximum(js[u], 0) for u in units}
        zs = {(t, d): _dot_nt(qs_ref[t * blk:(t + 1) * blk, :], kc_ref[jcs[t, d]]) for t, d in units}
        log_betas, sums = {}, {}
        for u in units:
            z = zs[u]
            sp = jnp.log(1.0 + jnp.exp2(jnp.abs(z) * -LOG2E))
            log_beta = jnp.minimum(z, 0.0) - sp
            log_keep = log_beta - z
            if diag[u]:
                log_keep = jnp.where(causal, log_keep, 0.0)
            log_betas[u] = log_beta
            hi, lo = _split_bf16(log_keep, 2)
            sums[u] = [_dot(jnp.concatenate([hi[:, h * blk:(h + 1) * blk], lo[:, h * blk:(h + 1) * blk]], axis=1),
                            mm_ref[...]) for h in range(2)]
        cmax = None
        for t in range(nsub):
            c = None if first else c_ref[t]
            acc = None if first else acc_ref[t]
            for d in range(count):
                u = (t, d)
                r0, r1 = sums[u]
                later = jnp.concatenate([r0[:, :blk], r1[:, :blk]], axis=1)
                total = jnp.concatenate([r0[:, blk:], r1[:, blk:]], axis=1)
                if diag[u]:
                    w = jnp.where(causal, jnp.exp(log_betas[u] + later), 0.0)
                    c = total
                else:
                    c = jnp.where(js[u] >= 0, c, -SB_NO_KEYS)
                    w = jnp.exp(log_betas[u] + later + c)
                    c = c + total
                pv = _dot(w.astype(BF16), vc_ref[jcs[u]])
                acc = pv if acc is None else acc + pv
            acc_ref[t] = acc
            c_ref[t] = c
            cmax = c if cmax is None else jnp.maximum(cmax, c)
        return jnp.max(cmax)

    last = qi * nsub + nsub - 1

    def cond(st):
        s, cm = st
        return jnp.logical_and(s <= last, cm > -SB_EXP_UNDERFLOW)

    def body(st):
        return st[0] + 1, visit(st[0], 1, False)

    lax.while_loop(cond, body, (jnp.int32(SB_FIRST_PASS), visit(0, SB_FIRST_PASS, True)))
    for t in range(nsub):
        o_ref[0, t * blk:(t + 1) * blk, :] = acc_ref[t].astype(o_ref.dtype)


def _sb_attention(qkv, b, s, tq):
    pairs = SB_HEADS * SB_HEAD_DIM // LANES
    nsub = tq // SB_BLOCK
    idx = np.arange(SB_BLOCK)
    later = (idx[:, None] > idx[None, :]).astype(np.float32)
    m = np.concatenate([later, np.ones((SB_BLOCK, SB_BLOCK), np.float32)], axis=1)
    mm = jnp.asarray(np.concatenate([m, m], axis=0), BF16)
    return pl.pallas_call(
        functools.partial(_sb_kernel, nsub=nsub),
        grid=(b, pairs, s // tq),
        in_specs=[
            pl.BlockSpec((1, tq, LANES), lambda bi, hp, qi: (bi, qi, hp)),
            pl.BlockSpec((1, s, LANES), lambda bi, hp, qi: (bi, 0, pairs + hp)),
            pl.BlockSpec((1, s, LANES), lambda bi, hp, qi: (bi, 0, 2 * pairs + hp)),
            pl.BlockSpec((2 * SB_BLOCK, 2 * SB_BLOCK), lambda bi, hp, qi: (0, 0)),
        ],
        out_specs=pl.BlockSpec((1, tq, LANES), lambda bi, hp, qi: (bi, qi, hp)),
        out_shape=jax.ShapeDtypeStruct((b, s, SB_HEADS * SB_HEAD_DIM), BF16),
        scratch_shapes=[pltpu.VMEM((tq, LANES), BF16),
                        pltpu.VMEM((s // SB_BLOCK, 2 * SB_BLOCK, LANES), BF16),
                        pltpu.VMEM((s // SB_BLOCK, 2 * SB_BLOCK, LANES), BF16),
                        pltpu.VMEM((nsub, SB_BLOCK, LANES), F32),
                        pltpu.VMEM((nsub, SB_BLOCK, 2 * SB_BLOCK), F32)],
        compiler_params=pltpu.CompilerParams(
            dimension_semantics=("arbitrary", "arbitrary", "arbitrary"), vmem_limit_bytes=VMEM_LIMIT),
        name="sb_attention",
    )(qkv, qkv, qkv, mm)


def _ssd_kernel(z_ref, xs_ref, bc_ref, dt_ref, dtt_ref, cw_ref, cb_ref,
                dtb_ref, dtbt_ref, alog_ref, alogt_ref, dexp_ref, gn_ref,
                tri_ref, trit_ref, e_ref,
                o_ref, prev_ref, state_ref, yout_ref):
    ck = SSD_CHUNK
    stride = SSD_STRIDE
    span = 8 * stride
    inner = SSD_HEADS * SSD_HEAD_DIM
    gw = inner // SSD_GROUPS
    gs = SSD_GROUPS * SSD_STATE
    nx = xs_ref.shape[1]
    tile_starts = [g * span + i for g in range(ck // span) for i in range(stride)]
    tile_rows = [pl.ds(start, 8, stride=stride) for start in tile_starts]
    ntile = len(tile_rows)

    @pl.when(pl.program_id(1) == 0)
    def _():
        prev_ref[...] = jnp.zeros_like(prev_ref)
        state_ref[...] = jnp.zeros_like(state_ref)

    first_row = lax.broadcasted_iota(jnp.int32, (8, LANES), 0) == 0
    lane = lax.broadcasted_iota(jnp.int32, (ck, LANES), 1)
    row = lax.broadcasted_iota(jnp.int32, (ck, LANES), 0)
    head0 = lane < SSD_HEAD_DIM
    tri_mask = _ssd_row_time(lane) <= _ssd_row_time(row)
    heads_per_group = SSD_HEADS // SSD_GROUPS

    def conv_silu(ch, c):
        lanes = slice(c * LANES, (c + 1) * LANES)
        src_ref, sc = (xs_ref, c) if c < nx else (bc_ref, c - nx)
        tiles = [src_ref[ch, sc, rows, :] for rows in tile_rows]

        def before(j):
            return prev_ref[j - 1, :, lanes] if ch == 0 else src_ref[ch - 1, sc, tile_rows[ntile - j], :]

        down = {-j: pltpu.roll(before(j), 1, 0) for j in range(1, SSD_CONV)}
        for n, tile in enumerate(tiles):
            if n % stride >= stride - (SSD_CONV - 1):
                down[n] = pltpu.roll(tile, 1, 0)
        wrapped = {}

        def back(n, k):
            if n % stride >= k:
                return tiles[n - k]
            if n - k not in wrapped:
                wrapped[n - k] = jnp.where(first_row, down[n - k], down[n - k + stride])
            return wrapped[n - k]

        out = []
        for n in range(ntile):
            acc = cb_ref[:, lanes]
            for k in range(SSD_CONV):
                acc = acc + back(n, k) * cw_ref[SSD_CONV - 1 - k:SSD_CONV - k, lanes]
            out.append(_silu(acc))
        if ch == z_ref.shape[0] - 1:
            for j in range(1, SSD_CONV):
                prev_ref[j - 1, :, lanes] = tiles[ntile - j]
        return jnp.concatenate(out, axis=0)

    for ch in range(z_ref.shape[0]):
        rows_ch = slice(ch * ck, (ch + 1) * ck)
        dt_raw = jnp.concatenate([dt_ref[pl.ds(ch * ck + start, 8, stride=stride), :] for start in tile_starts], axis=0)
        dt = _softplus(dt_raw + dtb_ref[...])
        dtt = _softplus(dtt_ref[:, rows_ch] + dtbt_ref[...])
        da = dt * (-jnp.exp(alog_ref[...]))
        dat = dtt * (-jnp.exp(alogt_ref[...]))
        a_cs = _dot_split_rhs(tri_ref[...], da, 3)
        a_cst = _dot_split_lhs(dat, trit_ref[...], 3)
        a_last = a_cs[ck - 1:ck, :]

        ea = jnp.exp(a_cs)
        dte = jnp.exp(a_last - a_cs)

        for g in range(SSD_GROUPS):
            cols = slice(g * gw, (g + 1) * gw)
            chunks = range(g * gw // LANES, (g + 1) * gw // LANES)
            xs = jnp.concatenate([conv_silu(ch, c) for c in chunks], axis=1)
            bg = conv_silu(ch, nx + g)
            cg16 = conv_silu(ch, nx + SSD_GROUPS + g).astype(BF16)
            e = e_ref[:, cols]
            ea_x = _dot_split_lhs(ea, e, 2)
            x_dt = xs * _dot_split_lhs(dt, e, 2)
            x_dt16 = x_dt.astype(BF16)
            x_end16 = (x_dt * _dot_split_lhs(dte, e, 2)).astype(BF16)

            cb = _dot_nt(cg16, bg.astype(BF16))
            y_pairs = []
            for rp in range(heads_per_group // 2):
                xp = x_dt16[:, rp * LANES:(rp + 1) * LANES]
                ys = []
                for hh in range(2):
                    r = g * heads_per_group + 2 * rp + hh
                    seg = a_cs[:, r:r + 1] - a_cst[r:r + 1, :]
                    lmat = cb * jnp.exp(jnp.where(tri_mask, seg, -jnp.inf))
                    ys.append(_dot(lmat.astype(BF16), xp))
                y_pairs.append(jnp.where(head0, ys[0], ys[1]))

            state = state_ref[:, cols]
            y_off = _dot(cg16, state.astype(BF16)) * ea_x
            contrib = _dot(bg.T.astype(BF16), x_end16)
            state_ref[:, cols] = state * ea_x[ck - 1:ck, :] + contrib

            y = jnp.concatenate(y_pairs, axis=1) + y_off + dexp_ref[:, cols] * xs
            for c in chunks:
                for n, rows in enumerate(tile_rows):
                    yout_ref[ch, c, rows, :] = y[8 * n:8 * n + 8, (c - chunks[0]) * LANES:(c - chunks[0] + 1) * LANES]
            y = jnp.concatenate([yout_ref[ch, c] for c in chunks], axis=1)
            y = y * jnp.concatenate([z_ref[ch, c] for c in chunks], axis=1)
            ms = jnp.mean(y * y, axis=-1, keepdims=True)
            o_ref[0, rows_ch, cols] = (y * lax.rsqrt(ms + EPS) * gn_ref[:, cols]).astype(o_ref.dtype)


def _ssd_branch(proj, dt, dtt, b, s, conv_w, conv_b, dt_bias, a_log, d_skip, ssd_norm):
    inner = SSD_HEADS * SSD_HEAD_DIM
    gs = SSD_GROUPS * SSD_STATE
    per = SSD_CHUNKS_PER_STEP
    nc = s // (SSD_CHUNK * per)
    nx, nbc = inner // LANES, 2 * gs // LANES
    chunked = lambda n, at: pl.BlockSpec((per, n, SSD_CHUNK, LANES), lambda bi, ci: (bi * nc + ci, at // n, 0, 0))
    idx = np.arange(SSD_CHUNK)
    when = _ssd_row_time(idx)
    tri = jnp.asarray((when[None, :] <= when[:, None]).astype(np.float32), BF16)
    trit = jnp.asarray((idx[:, None] <= when[None, :]).astype(np.float32), BF16)
    expand = np.zeros((LANES, inner), np.float32)
    expand[np.arange(inner) // SSD_HEAD_DIM, np.arange(inner)] = 1.0
    expand = jnp.asarray(expand, BF16)

    pad = LANES - SSD_HEADS
    row = lambda v: v.reshape(1, -1)
    dtb = jnp.pad(row(dt_bias), ((0, 0), (0, pad)))
    alog = jnp.pad(row(a_log), ((0, 0), (0, pad)))
    dexp = row(jnp.repeat(d_skip, SSD_HEAD_DIM))

    const = lambda shape: pl.BlockSpec(shape, lambda bi, ci: (0,) * len(shape))
    return pl.pallas_call(
        _ssd_kernel, grid=(b, nc),
        in_specs=[
            chunked(nx, 0),
            chunked(nx, nx),
            chunked(nbc, 2 * nx),
            pl.BlockSpec((per * SSD_CHUNK, LANES), lambda bi, ci: (bi * nc + ci, 0)),
            pl.BlockSpec((SSD_HEADS, per * SSD_CHUNK), lambda bi, ci: (0, bi * nc + ci)),
            const((SSD_CONV, inner + 2 * gs)), const((1, inner + 2 * gs)),
            const((1, LANES)), const((SSD_HEADS, 1)), const((1, LANES)), const((SSD_HEADS, 1)),
            const((1, inner)), const((1, inner)),
            const((SSD_CHUNK, SSD_CHUNK)), const((SSD_CHUNK, SSD_CHUNK)), const((LANES, inner)),
        ],
        out_specs=pl.BlockSpec((1, per * SSD_CHUNK, inner), lambda bi, ci: (bi, ci, 0)),
        out_shape=jax.ShapeDtypeStruct((b, s, inner), BF16),
        scratch_shapes=[pltpu.VMEM((SSD_CONV - 1, 8, inner + 2 * gs), F32),
                        pltpu.VMEM((SSD_STATE, inner), F32),
                        pltpu.VMEM((per, inner // LANES, SSD_CHUNK, LANES), F32)],
        compiler_params=pltpu.CompilerParams(
            dimension_semantics=("arbitrary", "arbitrary"), vmem_limit_bytes=VMEM_LIMIT),
        name="ssd_branch",
    )(proj, proj, proj, dt, dtt, conv_w, row(conv_b),
      dtb, dt_bias.reshape(-1, 1), alog, a_log.reshape(-1, 1), dexp, row(ssd_norm),
      tri, trit, expand)


def _mem_attn_kernel(q_ref, kv_ref, o_ref):
    width = MEM_HEADS * MEM_HEAD_DIM
    scale = MEM_HEAD_DIM ** -0.5
    head_cols = [slice(h * MEM_HEAD_DIM, (h + 1) * MEM_HEAD_DIM) for h in range(MEM_HEADS)]
    scores = [_dot_nt(q_ref[0, :, cols], kv_ref[0, :, cols]) * scale for cols in head_cols]
    probs = []
    for sc in scores:
        p = jnp.exp(sc - jnp.max(sc, axis=-1, keepdims=True))
        probs.append((p / jnp.sum(p, axis=-1, keepdims=True)).astype(BF16))
    for h, cols in enumerate(head_cols):
        v = kv_ref[0, :, width + h * MEM_HEAD_DIM:width + (h + 1) * MEM_HEAD_DIM]
        o_ref[0, :, cols] = _dot(probs[h], v).astype(o_ref.dtype)


def _mem_attention(qkv, kv, b, s, tq):
    width = MEM_HEADS * MEM_HEAD_DIM
    mlen = kv.shape[1]
    return pl.pallas_call(
        _mem_attn_kernel, grid=(b, s // tq),
        in_specs=[pl.BlockSpec((1, tq, width), lambda bi, qi: (bi, qi, 3)),
                  pl.BlockSpec((1, mlen, 2 * width), lambda bi, qi: (bi, 0, 0))],
        out_specs=pl.BlockSpec((1, tq, width), lambda bi, qi: (bi, qi, 0)),
        out_shape=jax.ShapeDtypeStruct((b, s, width), BF16),
        compiler_params=pltpu.CompilerParams(
            dimension_semantics=("arbitrary", "arbitrary"), vmem_limit_bytes=VMEM_LIMIT),
        name="mem_attention",
    )(qkv, kv)


def _merge_mlp_kernel(x_ref, ysb_ref, yssd_ref, ymem_ref, g0_ref, g1_ref, g2_ref,
                      wsb_ref, wssd_ref, wmem_ref, wo_ref, gmix_ref,
                      gpre_ref, wup_ref, wdown_ref, gpost_ref, o_ref, *, chunk):
    def gate(g_ref):
        logits = jnp.concatenate([jnp.concatenate([g_ref[r, c] for c in range(g_ref.shape[1])], axis=1)
                                  for r in range(g_ref.shape[0])], axis=0)
        return _sigmoid(logits)

    merged = (gate(g0_ref) * _dot(ysb_ref[...], wsb_ref[...])
              + gate(g1_ref) * _dot(yssd_ref[...], wssd_ref[...])
              + gate(g2_ref) * _dot(ymem_ref[...], wmem_ref[...]))
    mix = _dot(merged.astype(BF16), wo_ref[...])
    h = x_ref[...] + _rms_rows(mix, gmix_ref[...])

    u = _rms_rows(h, gpre_ref[...]).astype(BF16)
    nchunk = wup_ref.shape[1] // chunk
    up = lambda c: _dot(u, wup_ref[:, c * chunk:(c + 1) * chunk])
    ff = None
    hid_next = up(0)
    for c in range(nchunk):
        hid = hid_next
        if c + 1 < nchunk:
            hid_next = up(c + 1)
        act = jnp.square(jnp.maximum(hid, 0.0)).astype(BF16)
        t = _dot(act, wdown_ref[c * chunk:(c + 1) * chunk, :])
        ff = t if ff is None else ff + t
    o_ref[...] = h + _rms_rows(ff, gpost_ref[...])


def _merge_mlp(x, y_sb, y_ssd, y_mem, proj, w_sb, w_ssd, w_mem, w_o, g_mix,
               g_pre, w_up, w_down, g_post, tm):
    m, d = x.shape
    inner = y_ssd.shape[1]
    dc = d // LANES
    gate0 = proj.shape[1] // dc - N_GATES
    tile = lambda w: pl.BlockSpec((tm, w), lambda i: (i, 0))
    gate = lambda k: pl.BlockSpec((tm // LANES, dc, LANES, LANES), lambda i: (i, gate0 + k, 0, 0))
    full = lambda a: pl.BlockSpec(a.shape, lambda i: (0, 0), pipeline_mode=pl.Buffered(1))
    return pl.pallas_call(
        functools.partial(_merge_mlp_kernel, chunk=1024), grid=(m // tm,),
        in_specs=[tile(d), tile(d), tile(inner), tile(d), gate(0), gate(1), gate(2),
                  full(w_sb), full(w_ssd), full(w_mem), full(w_o), full(g_mix),
                  full(g_pre), full(w_up), full(w_down), full(g_post)],
        out_specs=tile(d),
        out_shape=jax.ShapeDtypeStruct((m, d), F32),
        compiler_params=pltpu.CompilerParams(
            dimension_semantics=("arbitrary",), vmem_limit_bytes=VMEM_LIMIT),
        name="merge_mlp",
    )(x, y_sb, y_ssd, y_mem, proj, proj, proj, w_sb, w_ssd, w_mem, w_o, g_mix,
      g_pre, w_up, w_down, g_post)


def _layer(h, mem, norm_mix_pre, w_in, conv_w, conv_b, dt_bias, a_log, d_skip, ssd_norm,
           norm_mem, w_mem_kv, w_sb_out, w_ssd_out, w_mem_out, w_o, norm_mix_post,
           norm_mlp_pre, w_up, w_down, norm_mlp_post):
    b, s, d = h.shape
    m = b * s
    row = lambda v: v.reshape(1, -1)
    x2 = h.reshape(m, d)

    sb_w = 3 * SB_HEADS * SB_HEAD_DIM
    inner = SSD_HEADS * SSD_HEAD_DIM
    conv_dim = inner + 2 * SSD_GROUPS * SSD_STATE
    o_z, o_xbc, o_dt = sb_w, sb_w + inner, sb_w + inner + conv_dim
    o_memq = o_dt + SSD_HEADS
    o_gate = o_memq + MEM_HEADS * MEM_HEAD_DIM

    tn = 1024
    w_t = jnp.swapaxes(w_in, 0, 1).astype(BF16)
    w_dt_t = jnp.pad(w_t[o_dt:o_memq], ((0, LANES - SSD_HEADS), (0, 0)))
    a_tiles = list(range(sb_w // tn)) + list(range(o_dt // tn, (o_dt + o_gate - o_memq) // tn))
    z_tiles = list(range(o_z // tn, o_xbc // tn))
    qkv, proj, dt, dtt = _in_proj(x2, row(norm_mix_pre), w_t, (o_dt, o_memq), a_tiles, z_tiles, w_dt_t,
                                  SSD_HEADS, 2048, tn)
    qkv = qkv.reshape(b, s, -1)

    y_sb = _sb_attention(qkv, b, s, 1024)
    y_ssd = _ssd_branch(proj, dt, dtt, b, s, conv_w, conv_b, dt_bias, a_log, d_skip, ssd_norm)

    mlen = mem.shape[1]
    kv = _norm_proj(mem.reshape(b * mlen, d), row(norm_mem), w_mem_kv.astype(BF16), BF16, b * mlen, 1024)
    y_mem = _mem_attention(qkv, kv.reshape(b, mlen, -1), b, s, 512)

    out = _merge_mlp(x2, y_sb.reshape(m, -1), y_ssd.reshape(m, -1), y_mem.reshape(m, -1), proj,
                     w_sb_out.astype(BF16), w_ssd_out.astype(BF16), w_mem_out.astype(BF16),
                     w_o.astype(BF16), row(norm_mix_post),
                     row(norm_mlp_pre), w_up.astype(BF16), w_down.astype(BF16), row(norm_mlp_post), 256)
    return out.reshape(b, s, d)


def kernel(x, mem, norm_mix_pre, w_in, conv_w, conv_b, dt_bias, a_log, d_skip, ssd_norm, norm_mem, w_mem_kv, w_sb_out, w_ssd_out, w_mem_out, w_o, norm_mix_post, norm_mlp_pre, w_up, w_down, norm_mlp_post):
    h = x
    for layer in range(w_in.shape[0]):
        h = _layer(h, mem, norm_mix_pre[layer], w_in[layer], conv_w[layer], conv_b[layer],
                   dt_bias[layer], a_log[layer], d_skip[layer], ssd_norm[layer], norm_mem[layer],
                   w_mem_kv[layer], w_sb_out[layer], w_ssd_out[layer], w_mem_out[layer], w_o[layer],
                   norm_mix_post[layer], norm_mlp_pre[layer], w_up[layer], w_down[layer],
                   norm_mlp_post[layer])
    return h
```

```python
import functools

import jax
import jax.numpy as jnp
import numpy as np
from jax import lax
from jax.experimental import pallas as pl
from jax.experimental.pallas import tpu as pltpu

F32 = jnp.float32
BF16 = jnp.bfloat16

EPS = 1e-6
LOG2E = 1.4426950408889634
LANES = 128
SB_HEADS = 16
SB_HEAD_DIM = 64
SB_BLOCK = 128
SSD_HEADS = 32
SSD_HEAD_DIM = 64
SSD_GROUPS = 4
SSD_STATE = 128
SSD_CHUNK = 128
SSD_CONV = 4
MEM_HEADS = 4
MEM_HEAD_DIM = 256
N_GATES = 3
VMEM_LIMIT = 56 * 1024 * 1024

SB_EXP_UNDERFLOW = 110.0
SB_NO_KEYS = 1e30


SSD_STRIDE = 4
SSD_CHUNKS_PER_STEP = 2


def _ssd_row_time(pos):
    span = 8 * SSD_STRIDE
    return (pos & -span) + (pos & 7) * SSD_STRIDE + ((pos >> 3) & (SSD_STRIDE - 1))


def _dot(a, b):
    return jnp.dot(a, b, preferred_element_type=F32)


def _dot_nt(a, b):
    return lax.dot_general(a, b, (((1,), (1,)), ((), ())), preferred_element_type=F32)


def _split_bf16(x, parts):
    out = []
    rem = x
    for _ in range(parts):
        p = rem.astype(BF16)
        out.append(p)
        rem = rem - p.astype(F32)
    return out


def _dot_split_lhs(x, m, parts):
    acc = None
    for p in _split_bf16(x, parts):
        t = _dot(p, m)
        acc = t if acc is None else acc + t
    return acc


def _dot_split_rhs(m, x, parts):
    acc = None
    for p in _split_bf16(x, parts):
        t = _dot(m, p)
        acc = t if acc is None else acc + t
    return acc


def _rms_rows(x, gain):
    ms = jnp.mean(x * x, axis=-1, keepdims=True)
    return x * lax.rsqrt(ms + EPS) * gain


def _softplus(x):
    return jnp.maximum(x, 0.0) + jnp.log1p(jnp.exp(-jnp.abs(x)))


def _sigmoid(x):
    return 1.0 / (1.0 + jnp.exp2(x * -LOG2E))


def _silu(x):
    h = 0.5 * x
    return h + h * jnp.tanh(h)


def _norm_proj_kernel(x_ref, g_ref, w_ref, o_ref, u_ref):
    @pl.when(pl.program_id(1) == 0)
    def _():
        u_ref[...] = _rms_rows(x_ref[...], g_ref[...]).astype(BF16)

    o_ref[...] = _dot(u_ref[...], w_ref[...]).astype(o_ref.dtype)


def _norm_proj(x, gain, w, out_dtype, tm, tn):
    m, d = x.shape
    n = w.shape[1]
    return pl.pallas_call(
        _norm_proj_kernel, grid=(m // tm, n // tn),
        in_specs=[pl.BlockSpec((tm, d), lambda i, j: (i, 0)),
                  pl.BlockSpec((1, d), lambda i, j: (0, 0)),
                  pl.BlockSpec((d, tn), lambda i, j: (0, j))],
        out_specs=pl.BlockSpec((tm, tn), lambda i, j: (i, j)),
        out_shape=jax.ShapeDtypeStruct((m, n), out_dtype),
        scratch_shapes=[pltpu.VMEM((tm, d), BF16)],
        compiler_params=pltpu.CompilerParams(dimension_semantics=("arbitrary", "arbitrary"),
                                             vmem_limit_bytes=VMEM_LIMIT),
        name="norm_proj",
    )(x, gain, w)


def _in_proj_kernel(x_ref, g_ref, wt_ref, wdtt_ref, oa_ref, oc_ref, dt_ref, dtt_ref, u_ref, *,
                    nt, a_tiles, silu_tiles, nh):
    j = pl.program_id(1)

    @pl.when(j == 0)
    def _():
        u = _rms_rows(x_ref[...], g_ref[...]).astype(BF16)
        u_ref[...] = u
        dt_ref[...] = _dot_nt(u, wdtt_ref[...])
        dtt_ref[...] = _dot_nt(wdtt_ref[0:nh, :], u)

    tile_in = lambda tiles: functools.reduce(jnp.logical_or, [j == t for t in tiles])
    plain_tiles = [t for t in range(nt) if t not in a_tiles and t not in silu_tiles]

    def store_chunked(res):
        for r in range(oc_ref.shape[0]):
            for c in range(oc_ref.shape[1]):
                oc_ref[r, c] = res[r * LANES:(r + 1) * LANES, c * LANES:(c + 1) * LANES]

    @pl.when(tile_in(a_tiles))
    def _():
        oa_ref[...] = _dot_nt(u_ref[...], wt_ref[...]).astype(oa_ref.dtype)

    @pl.when(tile_in(silu_tiles))
    def _():
        store_chunked(_silu(_dot_nt(u_ref[...], wt_ref[...])))

    @pl.when(tile_in(plain_tiles))
    def _():
        store_chunked(_dot_nt(u_ref[...], wt_ref[...]))


def _in_proj(x, gain, w_t, skip, a_tiles, silu_tiles, w_dt_t, nh, tm, tn):
    m, d = x.shape
    nt = (w_t.shape[0] - (skip[1] - skip[0])) // tn
    c_tiles = [t for t in range(nt) if t not in a_tiles]
    skip_tile, skip_rows = skip[0] // tn, skip[1] - skip[0]

    def rank(tiles, j):
        return jnp.maximum(sum((j >= t).astype(jnp.int32) for t in tiles) - 1, 0)

    return pl.pallas_call(
        functools.partial(_in_proj_kernel, nt=nt, a_tiles=tuple(a_tiles), silu_tiles=tuple(silu_tiles), nh=nh),
        grid=(m // tm, nt),
        in_specs=[pl.BlockSpec((tm, d), lambda i, j: (i, 0), pipeline_mode=pl.Buffered(1)),
                  pl.BlockSpec((1, d), lambda i, j: (0, 0)),
                  pl.BlockSpec((pl.Element(tn), pl.Element(d)),
                               lambda i, j: ((j * (tn // skip_rows) + (j >= skip_tile).astype(jnp.int32)) * skip_rows, 0)),
                  pl.BlockSpec((LANES, d), lambda i, j: (0, 0))],
        out_specs=[pl.BlockSpec((tm, tn), lambda i, j: (i, rank(a_tiles, j))),
                   pl.BlockSpec((tm // LANES, tn // LANES, LANES, LANES), lambda i, j: (i, rank(c_tiles, j), 0, 0)),
                   pl.BlockSpec((tm, LANES), lambda i, j: (i, 0)),
                   pl.BlockSpec((nh, tm), lambda i, j: (0, i))],
        out_shape=[jax.ShapeDtypeStruct((m, len(a_tiles) * tn), BF16),
                   jax.ShapeDtypeStruct((m // LANES, len(c_tiles) * tn // LANES, LANES, LANES), F32),
                   jax.ShapeDtypeStruct((m, LANES), F32),
                   jax.ShapeDtypeStruct((nh, m), F32)],
        scratch_shapes=[pltpu.VMEM((tm, d), BF16)],
        compiler_params=pltpu.CompilerParams(dimension_semantics=("arbitrary", "arbitrary"),
                                             vmem_limit_bytes=VMEM_LIMIT),
        name="in_proj",
    )(x, gain, w_t, w_dt_t)


def _sb_kernel(q_ref, k_ref, v_ref, mm_ref, o_ref, qs_ref, kc_ref, vc_ref, acc_ref, c_ref, *, nsub):
    qi = pl.program_id(2)
    blk = SB_BLOCK
    nblk = v_ref.shape[1] // blk
    head0 = lax.broadcasted_iota(jnp.int32, (blk, LANES), 1) < SB_HEAD_DIM
    key = jnp.bitwise_and(lax.broadcasted_iota(jnp.int32, (blk, 2 * blk), 1), blk - 1)
    causal = key < lax.broadcasted_iota(jnp.int32, (blk, 2 * blk), 0)
    scale = SB_HEAD_DIM ** -0.5

    def stack_heads(x):
        x = x.astype(F32)
        return jnp.concatenate([jnp.where(head0, x, 0.0), jnp.where(head0, 0.0, x)], axis=0).astype(BF16)

    @pl.when(qi == 0)
    def _():
        def fill(j, carry):
            off = pl.multiple_of(j * blk, blk)
            kc_ref[j] = stack_heads(k_ref[0, pl.ds(off, blk), :])
            vc_ref[j] = stack_heads(v_ref[0, pl.ds(off, blk), :])
            return carry
        lax.fori_loop(0, nblk, fill, 0)

    qs_ref[...] = (q_ref[0].astype(F32) * scale).astype(BF16)

    def visit(s, spans, first):
        units = [(t, i) for t in range(nsub) for i in range(len(spans))]
        dist = {u: spans[u[1]][0] for u in units}
        rows = {u: slice(spans[u[1]][1], spans[u[1]][2]) for u in units}
        diag = {u: first and dist[u] == 0 for u in units}
        js = {u: qi * nsub + u[0] - (s + dist[u]) for u in units}
        jcs = {u: js[u] if diag[u] else jnp.maximum(js[u], 0) for u in units}
        zs = {u: _dot_nt(qs_ref[u[0] * blk + rows[u].start:u[0] * blk + rows[u].stop, :], kc_ref[jcs[u]])
              for u in units}
        log_betas, sums = {}, {}
        for u in units:
            z = zs[u]
            sp = jnp.log(1.0 + jnp.exp2(jnp.abs(z) * -LOG2E))
            log_beta = jnp.minimum(z, 0.0) - sp
            log_keep = log_beta - z
            if diag[u]:
                log_keep = jnp.where(causal, log_keep, 0.0)
            log_betas[u] = log_beta
            hi, lo = _split_bf16(log_keep, 2)
            sums[u] = [_dot(jnp.concatenate([hi[:, h * blk:(h + 1) * blk], lo[:, h * blk:(h + 1) * blk]], axis=1),
                            mm_ref[...]) for h in range(2)]

        def put(full, part, r):
            pieces = ([full[:r.start]] if r.start > 0 else []) + [part]
            pieces += [full[r.stop:]] if r.stop < full.shape[0] else []
            return pieces[0] if len(pieces) == 1 else jnp.concatenate(pieces, axis=0)

        cmax = None
        for t in range(nsub):
            c = None if first else c_ref[t]
            acc = None if first else acc_ref[t]
            for i in range(len(spans)):
                u = (t, i)
                r0, r1 = sums[u]
                later = jnp.concatenate([r0[:, :blk], r1[:, :blk]], axis=1)
                total = jnp.concatenate([r0[:, blk:], r1[:, blk:]], axis=1)
                if diag[u]:
                    w = jnp.where(causal, jnp.exp(log_betas[u] + later), 0.0)
                    c = total
                    acc = _dot(w.astype(BF16), vc_ref[jcs[u]])
                else:
                    c_rows = jnp.where(js[u] >= 0, c[rows[u]], -SB_NO_KEYS)
                    w = jnp.exp(log_betas[u] + later + c_rows)
                    c = put(c, c_rows + total, rows[u])
                    acc = put(acc, acc[rows[u]] + _dot(w.astype(BF16), vc_ref[jcs[u]]), rows[u])
            acc_ref[t] = acc
            c_ref[t] = c
            cmax = c if cmax is None else jnp.maximum(cmax, c)
        return jnp.max(cmax)

    half = blk // 2
    last = qi * nsub + nsub - 1
    alive = lambda cm: cm > -SB_EXP_UNDERFLOW

    cm = visit(0, [(0, 0, blk), (1, 0, blk), (2, 0, half)], True)
    cm_low = jnp.max(functools.reduce(jnp.maximum, [c_ref[t, half:, :] for t in range(nsub)]))
    cm = lax.cond(alive(cm_low), lambda: visit(2, [(0, half, blk)], False), lambda: cm)

    def cond(st):
        return jnp.logical_and(st[0] <= last, alive(st[1]))

    def body(st):
        return st[0] + 1, visit(st[0], [(0, 0, blk)], False)

    lax.while_loop(cond, body, (jnp.int32(3), cm))
    for t in range(nsub):
        o_ref[0, t * blk:(t + 1) * blk, :] = acc_ref[t].astype(o_ref.dtype)


def _sb_attention(qkv, b, s, tq):
    pairs = SB_HEADS * SB_HEAD_DIM // LANES
    nsub = tq // SB_BLOCK
    idx = np.arange(SB_BLOCK)
    later = (idx[:, None] > idx[None, :]).astype(np.float32)
    m = np.concatenate([later, np.ones((SB_BLOCK, SB_BLOCK), np.float32)], axis=1)
    mm = jnp.asarray(np.concatenate([m, m], axis=0), BF16)
    return pl.pallas_call(
        functools.partial(_sb_kernel, nsub=nsub),
        grid=(b, pairs, s // tq),
        in_specs=[
            pl.BlockSpec((1, tq, LANES), lambda bi, hp, qi: (bi, qi, hp)),
            pl.BlockSpec((1, s, LANES), lambda bi, hp, qi: (bi, 0, pairs + hp)),
            pl.BlockSpec((1, s, LANES), lambda bi, hp, qi: (bi, 0, 2 * pairs + hp)),
            pl.BlockSpec((2 * SB_BLOCK, 2 * SB_BLOCK), lambda bi, hp, qi: (0, 0)),
        ],
        out_specs=pl.BlockSpec((1, tq, LANES), lambda bi, hp, qi: (bi, qi, hp)),
        out_shape=jax.ShapeDtypeStruct((b, s, SB_HEADS * SB_HEAD_DIM), BF16),
        scratch_shapes=[pltpu.VMEM((tq, LANES), BF16),
                        pltpu.VMEM((s // SB_BLOCK, 2 * SB_BLOCK, LANES), BF16),
                        pltpu.VMEM((s // SB_BLOCK, 2 * SB_BLOCK, LANES), BF16),
                        pltpu.VMEM((nsub, SB_BLOCK, LANES), F32),
                        pltpu.VMEM((nsub, SB_BLOCK, 2 * SB_BLOCK), F32)],
        compiler_params=pltpu.CompilerParams(
            dimension_semantics=("arbitrary", "arbitrary", "arbitrary"), vmem_limit_bytes=VMEM_LIMIT),
        name="sb_attention",
    )(qkv, qkv, qkv, mm)


def _ssd_kernel(z_ref, xs_ref, bc_ref, dt_ref, dtt_ref, cw_ref, cb_ref,
                dtb_ref, dtbt_ref, alog_ref, alogt_ref, dexp_ref, gn_ref,
                tri_ref, trit_ref, e_ref,
                o_ref, prev_ref, state_ref, yout_ref):
    ck = SSD_CHUNK
    stride = SSD_STRIDE
    span = 8 * stride
    inner = SSD_HEADS * SSD_HEAD_DIM
    gw = inner // SSD_GROUPS
    gs = SSD_GROUPS * SSD_STATE
    nx = xs_ref.shape[1]
    tile_starts = [g * span + i for g in range(ck // span) for i in range(stride)]
    tile_rows = [pl.ds(start, 8, stride=stride) for start in tile_starts]
    ntile = len(tile_rows)

    @pl.when(pl.program_id(1) == 0)
    def _():
        prev_ref[...] = jnp.zeros_like(prev_ref)
        state_ref[...] = jnp.zeros_like(state_ref)

    first_row = lax.broadcasted_iota(jnp.int32, (8, LANES), 0) == 0
    lane = lax.broadcasted_iota(jnp.int32, (ck, LANES), 1)
    row = lax.broadcasted_iota(jnp.int32, (ck, LANES), 0)
    head0 = lane < SSD_HEAD_DIM
    tri_mask = _ssd_row_time(lane) <= _ssd_row_time(row)
    heads_per_group = SSD_HEADS // SSD_GROUPS

    def conv_silu(ch, c):
        lanes = slice(c * LANES, (c + 1) * LANES)
        src_ref, sc = (xs_ref, c) if c < nx else (bc_ref, c - nx)
        tiles = [src_ref[ch, sc, rows, :] for rows in tile_rows]

        def before(j):
            return prev_ref[j - 1, :, lanes] if ch == 0 else src_ref[ch - 1, sc, tile_rows[ntile - j], :]

        down = {-j: pltpu.roll(before(j), 1, 0) for j in range(1, SSD_CONV)}
        for n, tile in enumerate(tiles):
            if n % stride >= stride - (SSD_CONV - 1):
                down[n] = pltpu.roll(tile, 1, 0)
        wrapped = {}

        def back(n, k):
            if n % stride >= k:
                return tiles[n - k]
            if n - k not in wrapped:
                wrapped[n - k] = jnp.where(first_row, down[n - k], down[n - k + stride])
            return wrapped[n - k]

        out = []
        for n in range(ntile):
            acc = cb_ref[:, lanes]
            for k in range(SSD_CONV):
                acc = acc + back(n, k) * cw_ref[SSD_CONV - 1 - k:SSD_CONV - k, lanes]
            out.append(_silu(acc))
        if ch == z_ref.shape[0] - 1:
            for j in range(1, SSD_CONV):
                prev_ref[j - 1, :, lanes] = tiles[ntile - j]
        return jnp.concatenate(out, axis=0)

    for ch in range(z_ref.shape[0]):
        rows_ch = slice(ch * ck, (ch + 1) * ck)
        dt_raw = jnp.concatenate([dt_ref[pl.ds(ch * ck + start, 8, stride=stride), :] for start in tile_starts], axis=0)
        dt = _softplus(dt_raw + dtb_ref[...])
        dtt = _softplus(dtt_ref[:, rows_ch] + dtbt_ref[...])
        da = dt * (-jnp.exp(alog_ref[...]))
        dat = dtt * (-jnp.exp(alogt_ref[...]))
        a_cs = _dot_split_rhs(tri_ref[...], da, 3)
        a_cst = _dot_split_lhs(dat, trit_ref[...], 3)
        a_last = a_cs[ck - 1:ck, :]

        ea = jnp.exp(a_cs)
        dte = jnp.exp(a_last - a_cs)

        for g in range(SSD_GROUPS):
            cols = slice(g * gw, (g + 1) * gw)
            chunks = range(g * gw // LANES, (g + 1) * gw // LANES)
            xs = jnp.concatenate([conv_silu(ch, c) for c in chunks], axis=1)
            bg = conv_silu(ch, nx + g)
            cg16 = conv_silu(ch, nx + SSD_GROUPS + g).astype(BF16)
            e = e_ref[:, cols]
            ea_x = _dot_split_lhs(ea, e, 2)
            x_dt = xs * _dot_split_lhs(dt, e, 2)
            x_dt16 = x_dt.astype(BF16)
            x_end16 = (x_dt * _dot_split_lhs(dte, e, 2)).astype(BF16)

            cb = _dot_nt(cg16, bg.astype(BF16))
            y_pairs = []
            for rp in range(heads_per_group // 2):
                xp = x_dt16[:, rp * LANES:(rp + 1) * LANES]
                ys = []
                for hh in range(2):
                    r = g * heads_per_group + 2 * rp + hh
                    seg = a_cs[:, r:r + 1] - a_cst[r:r + 1, :]
                    lmat = cb * jnp.exp(jnp.where(tri_mask, seg, -jnp.inf))
                    ys.append(_dot(lmat.astype(BF16), xp))
                y_pairs.append(jnp.where(head0, ys[0], ys[1]))

            state = state_ref[:, cols]
            y_off = _dot(cg16, state.astype(BF16)) * ea_x
            contrib = _dot(bg.T.astype(BF16), x_end16)
            state_ref[:, cols] = state * ea_x[ck - 1:ck, :] + contrib

            y = jnp.concatenate(y_pairs, axis=1) + y_off + dexp_ref[:, cols] * xs
            for c in chunks:
                for n, rows in enumerate(tile_rows):
                    yout_ref[ch, c, rows, :] = y[8 * n:8 * n + 8, (c - chunks[0]) * LANES:(c - chunks[0] + 1) * LANES]
            y = jnp.concatenate([yout_ref[ch, c] for c in chunks], axis=1)
            y = y * jnp.concatenate([z_ref[ch, c] for c in chunks], axis=1)
            ms = jnp.mean(y * y, axis=-1, keepdims=True)
            o_ref[0, rows_ch, cols] = (y * lax.rsqrt(ms + EPS) * gn_ref[:, cols]).astype(o_ref.dtype)


def _ssd_branch(proj, dt, dtt, b, s, conv_w, conv_b, dt_bias, a_log, d_skip, ssd_norm):
    inner = SSD_HEADS * SSD_HEAD_DIM
    gs = SSD_GROUPS * SSD_STATE
    per = SSD_CHUNKS_PER_STEP
    nc = s // (SSD_CHUNK * per)
    nx, nbc = inner // LANES, 2 * gs // LANES
    chunked = lambda n, at: pl.BlockSpec((per, n, SSD_CHUNK, LANES), lambda bi, ci: (bi * nc + ci, at // n, 0, 0))
    idx = np.arange(SSD_CHUNK)
    when = _ssd_row_time(idx)
    tri = jnp.asarray((when[None, :] <= when[:, None]).astype(np.float32), BF16)
    trit = jnp.asarray((idx[:, None] <= when[None, :]).astype(np.float32), BF16)
    expand = np.zeros((LANES, inner), np.float32)
    expand[np.arange(inner) // SSD_HEAD_DIM, np.arange(inner)] = 1.0
    expand = jnp.asarray(expand, BF16)

    pad = LANES - SSD_HEADS
    row = lambda v: v.reshape(1, -1)
    dtb = jnp.pad(row(dt_bias), ((0, 0), (0, pad)))
    alog = jnp.pad(row(a_log), ((0, 0), (0, pad)))
    dexp = row(jnp.repeat(d_skip, SSD_HEAD_DIM))

    const = lambda shape: pl.BlockSpec(shape, lambda bi, ci: (0,) * len(shape))
    return pl.pallas_call(
        _ssd_kernel, grid=(b, nc),
        in_specs=[
            chunked(nx, 0),
            chunked(nx, nx),
            chunked(nbc, 2 * nx),
            pl.BlockSpec((per * SSD_CHUNK, LANES), lambda bi, ci: (bi * nc + ci, 0)),
            pl.BlockSpec((SSD_HEADS, per * SSD_CHUNK), lambda bi, ci: (0, bi * nc + ci)),
            const((SSD_CONV, inner + 2 * gs)), const((1, inner + 2 * gs)),
            const((1, LANES)), const((SSD_HEADS, 1)), const((1, LANES)), const((SSD_HEADS, 1)),
            const((1, inner)), const((1, inner)),
            const((SSD_CHUNK, SSD_CHUNK)), const((SSD_CHUNK, SSD_CHUNK)), const((LANES, inner)),
        ],
        out_specs=pl.BlockSpec((1, per * SSD_CHUNK, inner), lambda bi, ci: (bi, ci, 0)),
        out_shape=jax.ShapeDtypeStruct((b, s, inner), BF16),
        scratch_shapes=[pltpu.VMEM((SSD_CONV - 1, 8, inner + 2 * gs), F32),
                        pltpu.VMEM((SSD_STATE, inner), F32),
                        pltpu.VMEM((per, inner // LANES, SSD_CHUNK, LANES), F32)],
        compiler_params=pltpu.CompilerParams(
            dimension_semantics=("arbitrary", "arbitrary"), vmem_limit_bytes=VMEM_LIMIT),
        name="ssd_branch",
    )(proj, proj, proj, dt, dtt, conv_w, row(conv_b),
      dtb, dt_bias.reshape(-1, 1), alog, a_log.reshape(-1, 1), dexp, row(ssd_norm),
      tri, trit, expand)


def _mem_attn_kernel(q_ref, kv_ref, o_ref):
    width = MEM_HEADS * MEM_HEAD_DIM
    scale = MEM_HEAD_DIM ** -0.5
    head_cols = [slice(h * MEM_HEAD_DIM, (h + 1) * MEM_HEAD_DIM) for h in range(MEM_HEADS)]
    scores = [_dot_nt(q_ref[0, :, cols], kv_ref[0, :, cols]) * scale for cols in head_cols]
    probs = []
    for sc in scores:
        p = jnp.exp(sc - jnp.max(sc, axis=-1, keepdims=True))
        probs.append((p / jnp.sum(p, axis=-1, keepdims=True)).astype(BF16))
    for h, cols in enumerate(head_cols):
        v = kv_ref[0, :, width + h * MEM_HEAD_DIM:width + (h + 1) * MEM_HEAD_DIM]
        o_ref[0, :, cols] = _dot(probs[h], v).astype(o_ref.dtype)


def _mem_attention(qkv, kv, b, s, tq):
    width = MEM_HEADS * MEM_HEAD_DIM
    mlen = kv.shape[1]
    return pl.pallas_call(
        _mem_attn_kernel, grid=(b, s // tq),
        in_specs=[pl.BlockSpec((1, tq, width), lambda bi, qi: (bi, qi, 3)),
                  pl.BlockSpec((1, mlen, 2 * width), lambda bi, qi: (bi, 0, 0))],
        out_specs=pl.BlockSpec((1, tq, width), lambda bi, qi: (bi, qi, 0)),
        out_shape=jax.ShapeDtypeStruct((b, s, width), BF16),
        compiler_params=pltpu.CompilerParams(
            dimension_semantics=("arbitrary", "arbitrary"), vmem_limit_bytes=VMEM_LIMIT),
        name="mem_attention",
    )(qkv, kv)


def _merge_mlp_kernel(x_ref, ysb_ref, yssd_ref, ymem_ref, g0_ref, g1_ref, g2_ref,
                      wsb_ref, wssd_ref, wmem_ref, wo_ref, gmix_ref,
                      gpre_ref, wup_ref, wdown_ref, gpost_ref, o_ref, *, chunk):
    def rows_of(part):
        rows = slice(part.start * LANES, part.stop * LANES)

        def gate(g_ref):
            logits = jnp.concatenate([jnp.concatenate([g_ref[r, c] for c in range(g_ref.shape[1])], axis=1)
                                      for r in part], axis=0)
            return _sigmoid(logits)

        merged = (gate(g0_ref) * _dot(ysb_ref[rows, :], wsb_ref[...])
                  + gate(g1_ref) * _dot(yssd_ref[rows, :], wssd_ref[...])
                  + gate(g2_ref) * _dot(ymem_ref[rows, :], wmem_ref[...]))
        yield
        mix = _dot(merged.astype(BF16), wo_ref[...])
        h = x_ref[rows, :] + _rms_rows(mix, gmix_ref[...])
        u = _rms_rows(h, gpre_ref[...]).astype(BF16)
        yield
        ff = None
        for c in range(wup_ref.shape[1] // chunk):
            hid = _dot(u, wup_ref[:, c * chunk:(c + 1) * chunk])
            act = jnp.square(jnp.maximum(hid, 0.0)).astype(BF16)
            t = _dot(act, wdown_ref[c * chunk:(c + 1) * chunk, :])
            ff = t if ff is None else ff + t
            yield
        o_ref[rows, :] = h + _rms_rows(ff, gpost_ref[...])

    nrow = g0_ref.shape[0]
    pending = [rows_of(range(0, nrow // 2)), rows_of(range(nrow // 2, nrow))]
    while pending:
        pending = [p for p in pending if next(p, "done") != "done"]


def _merge_mlp(x, y_sb, y_ssd, y_mem, proj, w_sb, w_ssd, w_mem, w_o, g_mix,
               g_pre, w_up, w_down, g_post, tm):
    m, d = x.shape
    inner = y_ssd.shape[1]
    dc = d // LANES
    gate0 = proj.shape[1] // dc - N_GATES
    tile = lambda w: pl.BlockSpec((tm, w), lambda i: (i, 0))
    gate = lambda k: pl.BlockSpec((tm // LANES, dc, LANES, LANES), lambda i: (i, gate0 + k, 0, 0))
    full = lambda a: pl.BlockSpec(a.shape, lambda i: (0, 0), pipeline_mode=pl.Buffered(1))
    return pl.pallas_call(
        functools.partial(_merge_mlp_kernel, chunk=1024), grid=(m // tm,),
        in_specs=[tile(d), tile(d), tile(inner), tile(d), gate(0), gate(1), gate(2),
                  full(w_sb), full(w_ssd), full(w_mem), full(w_o), full(g_mix),
                  full(g_pre), full(w_up), full(w_down), full(g_post)],
        out_specs=tile(d),
        out_shape=jax.ShapeDtypeStruct((m, d), F32),
        compiler_params=pltpu.CompilerParams(
            dimension_semantics=("arbitrary",), vmem_limit_bytes=VMEM_LIMIT),
        name="merge_mlp",
    )(x, y_sb, y_ssd, y_mem, proj, proj, proj, w_sb, w_ssd, w_mem, w_o, g_mix,
      g_pre, w_up, w_down, g_post)


def _layer(h, mem, norm_mix_pre, w_in, conv_w, conv_b, dt_bias, a_log, d_skip, ssd_norm,
           norm_mem, w_mem_kv, w_sb_out, w_ssd_out, w_mem_out, w_o, norm_mix_post,
           norm_mlp_pre, w_up, w_down, norm_mlp_post):
    b, s, d = h.shape
    m = b * s
    row = lambda v: v.reshape(1, -1)
    x2 = h.reshape(m, d)

    sb_w = 3 * SB_HEADS * SB_HEAD_DIM
    inner = SSD_HEADS * SSD_HEAD_DIM
    conv_dim = inner + 2 * SSD_GROUPS * SSD_STATE
    o_z, o_xbc, o_dt = sb_w, sb_w + inner, sb_w + inner + conv_dim
    o_memq = o_dt + SSD_HEADS
    o_gate = o_memq + MEM_HEADS * MEM_HEAD_DIM

    tn = 1024
    w_t = jnp.swapaxes(w_in, 0, 1).astype(BF16)
    w_dt_t = jnp.pad(w_t[o_dt:o_memq], ((0, LANES - SSD_HEADS), (0, 0)))
    a_tiles = list(range(sb_w // tn)) + list(range(o_dt // tn, (o_dt + o_gate - o_memq) // tn))
    z_tiles = list(range(o_z // tn, o_xbc // tn))
    qkv, proj, dt, dtt = _in_proj(x2, row(norm_mix_pre), w_t, (o_dt, o_memq), a_tiles, z_tiles, w_dt_t,
                                  SSD_HEADS, 2048, tn)
    qkv = qkv.reshape(b, s, -1)

    y_sb = _sb_attention(qkv, b, s, 1024)
    y_ssd = _ssd_branch(proj, dt, dtt, b, s, conv_w, conv_b, dt_bias, a_log, d_skip, ssd_norm)

    mlen = mem.shape[1]
    kv = _norm_proj(mem.reshape(b * mlen, d), row(norm_mem), w_mem_kv.astype(BF16), BF16, b * mlen, 1024)
    y_mem = _mem_attention(qkv, kv.reshape(b, mlen, -1), b, s, 512)

    out = _merge_mlp(x2, y_sb.reshape(m, -1), y_ssd.reshape(m, -1), y_mem.reshape(m, -1), proj,
                     w_sb_out.astype(BF16), w_ssd_out.astype(BF16), w_mem_out.astype(BF16),
                     w_o.astype(BF16), row(norm_mix_post),
                     row(norm_mlp_pre), w_up.astype(BF16), w_down.astype(BF16), row(norm_mlp_post), 256)
    return out.reshape(b, s, d)


def kernel(x, mem, norm_mix_pre, w_in, conv_w, conv_b, dt_bias, a_log, d_skip, ssd_norm, norm_mem, w_mem_kv, w_sb_out, w_ssd_out, w_mem_out, w_o, norm_mix_post, norm_mlp_pre, w_up, w_down, norm_mlp_post):
    h = x
    for layer in range(w_in.shape[0]):
        h = _layer(h, mem, norm_mix_pre[layer], w_in[layer], conv_w[layer], conv_b[layer],
                   dt_bias[layer], a_log[layer], d_skip[layer], ssd_norm[layer], norm_mem[layer],
                   w_mem_kv[layer], w_sb_out[layer], w_ssd_out[layer], w_mem_out[layer], w_o[layer],
                   norm_mix_post[layer], norm_mlp_pre[layer], w_up[layer], w_down[layer],
                   norm_mlp_post[layer])
    return h
```

```python
import functools

import jax
import jax.numpy as jnp
import numpy as np
from jax import lax
from jax.experimental import pallas as pl
from jax.experimental.pallas import tpu as pltpu

F32 = jnp.float32
BF16 = jnp.bfloat16

EPS = 1e-6
LOG2E = 1.4426950408889634
LANES = 128
SB_HEADS = 16
SB_HEAD_DIM = 64
SB_BLOCK = 128
SSD_HEADS = 32
SSD_HEAD_DIM = 64
SSD_GROUPS = 4
SSD_STATE = 128
SSD_CHUNK = 128
SSD_CONV = 4
MEM_HEADS = 4
MEM_HEAD_DIM = 256
N_GATES = 3
VMEM_LIMIT = 56 * 1024 * 1024

SB_EXP_UNDERFLOW = 110.0
SB_NO_KEYS = 1e30


SSD_STRIDE = 4
SSD_CHUNKS_PER_STEP = 4


def _ssd_row_time(pos):
    span = 8 * SSD_STRIDE
    return (pos & -span) + (pos & 7) * SSD_STRIDE + ((pos >> 3) & (SSD_STRIDE - 1))


def _dot(a, b):
    return jnp.dot(a, b, preferred_element_type=F32)


def _dot_nt(a, b):
    return lax.dot_general(a, b, (((1,), (1,)), ((), ())), preferred_element_type=F32)


def _split_bf16(x, parts):
    out = []
    rem = x
    for _ in range(parts):
        p = rem.astype(BF16)
        out.append(p)
        rem = rem - p.astype(F32)
    return out


def _dot_split_lhs(x, m, parts):
    acc = None
    for p in _split_bf16(x, parts):
        t = _dot(p, m)
        acc = t if acc is None else acc + t
    return acc


def _dot_split_rhs(m, x, parts):
    acc = None
    for p in _split_bf16(x, parts):
        t = _dot(m, p)
        acc = t if acc is None else acc + t
    return acc


def _rms_rows(x, gain):
    ms = jnp.mean(x * x, axis=-1, keepdims=True)
    return x * lax.rsqrt(ms + EPS) * gain


def _softplus(x):
    return jnp.maximum(x, 0.0) + jnp.log1p(jnp.exp(-jnp.abs(x)))


def _sigmoid(x):
    return 1.0 / (1.0 + jnp.exp2(x * -LOG2E))


def _silu(x):
    h = 0.5 * x
    return h + h * jnp.tanh(h)


def _norm_proj_kernel(x_ref, g_ref, w_ref, o_ref, u_ref):
    @pl.when(pl.program_id(1) == 0)
    def _():
        u_ref[...] = _rms_rows(x_ref[...], g_ref[...]).astype(BF16)

    o_ref[...] = _dot(u_ref[...], w_ref[...]).astype(o_ref.dtype)


def _norm_proj(x, gain, w, out_dtype, tm, tn):
    m, d = x.shape
    n = w.shape[1]
    return pl.pallas_call(
        _norm_proj_kernel, grid=(m // tm, n // tn),
        in_specs=[pl.BlockSpec((tm, d), lambda i, j: (i, 0)),
                  pl.BlockSpec((1, d), lambda i, j: (0, 0)),
                  pl.BlockSpec((d, tn), lambda i, j: (0, j))],
        out_specs=pl.BlockSpec((tm, tn), lambda i, j: (i, j)),
        out_shape=jax.ShapeDtypeStruct((m, n), out_dtype),
        scratch_shapes=[pltpu.VMEM((tm, d), BF16)],
        compiler_params=pltpu.CompilerParams(dimension_semantics=("arbitrary", "arbitrary"),
                                             vmem_limit_bytes=VMEM_LIMIT),
        name="norm_proj",
    )(x, gain, w)


def _in_proj_kernel(x_ref, g_ref, wt_ref, wdtt_ref, oa_ref, oc_ref, dt_ref, dtt_ref, u_ref, *,
                    nt, a_tiles, silu_tiles, nh):
    j = pl.program_id(1)

    @pl.when(j == 0)
    def _():
        u = _rms_rows(x_ref[...], g_ref[...]).astype(BF16)
        u_ref[...] = u
        dt_ref[...] = _dot_nt(u, wdtt_ref[...])
        dtt_ref[...] = _dot_nt(wdtt_ref[0:nh, :], u)

    tile_in = lambda tiles: functools.reduce(jnp.logical_or, [j == t for t in tiles])
    plain_tiles = [t for t in range(nt) if t not in a_tiles and t not in silu_tiles]

    def store_chunked(res):
        for r in range(oc_ref.shape[0]):
            for c in range(oc_ref.shape[1]):
                oc_ref[r, c] = res[r * LANES:(r + 1) * LANES, c * LANES:(c + 1) * LANES]

    @pl.when(tile_in(a_tiles))
    def _():
        oa_ref[...] = _dot_nt(u_ref[...], wt_ref[...]).astype(oa_ref.dtype)

    @pl.when(tile_in(silu_tiles))
    def _():
        store_chunked(_silu(_dot_nt(u_ref[...], wt_ref[...])))

    @pl.when(tile_in(plain_tiles))
    def _():
        store_chunked(_dot_nt(u_ref[...], wt_ref[...]))


def _in_proj(x, gain, w_t, skip, a_tiles, silu_tiles, w_dt_t, nh, tm, tn):
    m, d = x.shape
    nt = (w_t.shape[0] - (skip[1] - skip[0])) // tn
    c_tiles = [t for t in range(nt) if t not in a_tiles]
    skip_tile, skip_rows = skip[0] // tn, skip[1] - skip[0]

    def rank(tiles, j):
        return jnp.maximum(sum((j >= t).astype(jnp.int32) for t in tiles) - 1, 0)

    return pl.pallas_call(
        functools.partial(_in_proj_kernel, nt=nt, a_tiles=tuple(a_tiles), silu_tiles=tuple(silu_tiles), nh=nh),
        grid=(m // tm, nt),
        in_specs=[pl.BlockSpec((tm, d), lambda i, j: (i, 0), pipeline_mode=pl.Buffered(1)),
                  pl.BlockSpec((1, d), lambda i, j: (0, 0)),
                  pl.BlockSpec((pl.Element(tn), pl.Element(d)),
                               lambda i, j: ((j * (tn // skip_rows) + (j >= skip_tile).astype(jnp.int32)) * skip_rows, 0)),
                  pl.BlockSpec((LANES, d), lambda i, j: (0, 0))],
        out_specs=[pl.BlockSpec((tm, tn), lambda i, j: (i, rank(a_tiles, j))),
                   pl.BlockSpec((tm // LANES, tn // LANES, LANES, LANES), lambda i, j: (i, rank(c_tiles, j), 0, 0)),
                   pl.BlockSpec((tm, LANES), lambda i, j: (i, 0)),
                   pl.BlockSpec((nh, tm), lambda i, j: (0, i))],
        out_shape=[jax.ShapeDtypeStruct((m, len(a_tiles) * tn), BF16),
                   jax.ShapeDtypeStruct((m // LANES, len(c_tiles) * tn // LANES, LANES, LANES), F32),
                   jax.ShapeDtypeStruct((m, LANES), F32),
                   jax.ShapeDtypeStruct((nh, m), F32)],
        scratch_shapes=[pltpu.VMEM((tm, d), BF16)],
        compiler_params=pltpu.CompilerParams(dimension_semantics=("arbitrary", "arbitrary"),
                                             vmem_limit_bytes=VMEM_LIMIT),
        name="in_proj",
    )(x, gain, w_t, w_dt_t)


def _sb_kernel(q_ref, k_ref, v_ref, mm_ref, o_ref, qs_ref, kc_ref, vc_ref, acc_ref, c_ref, *, nsub):
    qi = pl.program_id(2)
    blk = SB_BLOCK
    nblk = v_ref.shape[1] // blk
    head0 = lax.broadcasted_iota(jnp.int32, (blk, LANES), 1) < SB_HEAD_DIM
    key = jnp.bitwise_and(lax.broadcasted_iota(jnp.int32, (blk, 2 * blk), 1), blk - 1)
    causal = key < lax.broadcasted_iota(jnp.int32, (blk, 2 * blk), 0)
    scale = SB_HEAD_DIM ** -0.5

    keep0 = jnp.where(head0, 1.0, 0.0).astype(BF16)
    keep1 = jnp.where(head0, 0.0, 1.0).astype(BF16)

    def stack_heads(x):
        return jnp.concatenate([x * keep0, x * keep1], axis=0)

    @pl.when(qi == 0)
    def _():
        def fill(j, carry):
            off = pl.multiple_of(j * blk, blk)
            kc_ref[j] = stack_heads(k_ref[0, pl.ds(off, blk), :])
            vc_ref[j] = stack_heads(v_ref[0, pl.ds(off, blk), :])
            return carry
        lax.fori_loop(0, nblk, fill, 0)

    qs_ref[...] = (q_ref[0].astype(F32) * scale).astype(BF16)

    def visit(s, spans, first):
        units = [(t, i) for t in range(nsub) for i in range(len(spans))]
        dist = {u: spans[u[1]][0] for u in units}
        rows = {u: slice(spans[u[1]][1], spans[u[1]][2]) for u in units}
        diag = {u: first and dist[u] == 0 for u in units}
        js = {u: qi * nsub + u[0] - (s + dist[u]) for u in units}
        jcs = {u: js[u] if diag[u] else jnp.maximum(js[u], 0) for u in units}
        zs = {u: _dot_nt(qs_ref[u[0] * blk + rows[u].start:u[0] * blk + rows[u].stop, :], kc_ref[jcs[u]])
              for u in units}
        log_betas, sums = {}, {}
        for u in units:
            z = zs[u]
            sp = jnp.log(1.0 + jnp.exp2(jnp.abs(z) * -LOG2E))
            log_beta = jnp.minimum(z, 0.0) - sp
            log_keep = log_beta - z
            if diag[u]:
                log_keep = jnp.where(causal, log_keep, 0.0)
            log_betas[u] = log_beta
            hi, lo = _split_bf16(log_keep, 2)
            sums[u] = [_dot(jnp.concatenate([hi[:, h * blk:(h + 1) * blk], lo[:, h * blk:(h + 1) * blk]], axis=1),
                            mm_ref[...]) for h in range(2)]

        def put(full, part, r):
            pieces = ([full[:r.start]] if r.start > 0 else []) + [part]
            pieces += [full[r.stop:]] if r.stop < full.shape[0] else []
            return pieces[0] if len(pieces) == 1 else jnp.concatenate(pieces, axis=0)

        cmax = None
        for t in range(nsub):
            c = None if first else c_ref[t]
            acc = None if first else acc_ref[t]
            for i in range(len(spans)):
                u = (t, i)
                r0, r1 = sums[u]
                later = jnp.concatenate([r0[:, :blk], r1[:, :blk]], axis=1)
                total = jnp.concatenate([r0[:, blk:], r1[:, blk:]], axis=1)
                if diag[u]:
                    w = jnp.where(causal, jnp.exp(log_betas[u] + later), 0.0)
                    c = total
                    acc = _dot(w.astype(BF16), vc_ref[jcs[u]])
                else:
                    c_rows = jnp.where(js[u] >= 0, c[rows[u]], -SB_NO_KEYS)
                    w = jnp.exp(log_betas[u] + later + c_rows)
                    c = put(c, c_rows + total, rows[u])
                    acc = put(acc, acc[rows[u]] + _dot(w.astype(BF16), vc_ref[jcs[u]]), rows[u])
            acc_ref[t] = acc
            c_ref[t] = c
            cmax = c if cmax is None else jnp.maximum(cmax, c)
        return jnp.max(cmax)

    half = blk // 2
    last = qi * nsub + nsub - 1
    alive = lambda cm: cm > -SB_EXP_UNDERFLOW

    cm = visit(0, [(0, 0, blk), (1, 0, blk), (2, 0, half)], True)
    cm_low = jnp.max(functools.reduce(jnp.maximum, [c_ref[t, half:, :] for t in range(nsub)]))
    cm = lax.cond(alive(cm_low), lambda: visit(2, [(0, half, blk)], False), lambda: cm)

    def cond(st):
        return jnp.logical_and(st[0] <= last, alive(st[1]))

    def body(st):
        return st[0] + 1, visit(st[0], [(0, 0, blk)], False)

    lax.while_loop(cond, body, (jnp.int32(3), cm))
    for t in range(nsub):
        o_ref[0, t * blk:(t + 1) * blk, :] = acc_ref[t].astype(o_ref.dtype)


def _sb_attention(qkv, b, s, tq):
    pairs = SB_HEADS * SB_HEAD_DIM // LANES
    nsub = tq // SB_BLOCK
    idx = np.arange(SB_BLOCK)
    later = (idx[:, None] > idx[None, :]).astype(np.float32)
    m = np.concatenate([later, np.ones((SB_BLOCK, SB_BLOCK), np.float32)], axis=1)
    mm = jnp.asarray(np.concatenate([m, m], axis=0), BF16)
    return pl.pallas_call(
        functools.partial(_sb_kernel, nsub=nsub),
        grid=(b, pairs, s // tq),
        in_specs=[
            pl.BlockSpec((1, tq, LANES), lambda bi, hp, qi: (bi, qi, hp)),
            pl.BlockSpec((1, s, LANES), lambda bi, hp, qi: (bi, 0, pairs + hp)),
            pl.BlockSpec((1, s, LANES), lambda bi, hp, qi: (bi, 0, 2 * pairs + hp)),
            pl.BlockSpec((2 * SB_BLOCK, 2 * SB_BLOCK), lambda bi, hp, qi: (0, 0)),
        ],
        out_specs=pl.BlockSpec((1, tq, LANES), lambda bi, hp, qi: (bi, qi, hp)),
        out_shape=jax.ShapeDtypeStruct((b, s, SB_HEADS * SB_HEAD_DIM), BF16),
        scratch_shapes=[pltpu.VMEM((tq, LANES), BF16),
                        pltpu.VMEM((s // SB_BLOCK, 2 * SB_BLOCK, LANES), BF16),
                        pltpu.VMEM((s // SB_BLOCK, 2 * SB_BLOCK, LANES), BF16),
                        pltpu.VMEM((nsub, SB_BLOCK, LANES), F32),
                        pltpu.VMEM((nsub, SB_BLOCK, 2 * SB_BLOCK), F32)],
        compiler_params=pltpu.CompilerParams(
            dimension_semantics=("arbitrary", "arbitrary", "arbitrary"), vmem_limit_bytes=VMEM_LIMIT),
        name="sb_attention",
    )(qkv, qkv, qkv, mm)


def _ssd_kernel(z_ref, xs_ref, bc_ref, dt_ref, dtt_ref, cw_ref, cb_ref,
                dtb_ref, dtbt_ref, alog_ref, alogt_ref, dexp_ref, gn_ref,
                tri_ref, trit_ref, e_ref,
                o_ref, prev_ref, state_ref, yout_ref):
    ck = SSD_CHUNK
    stride = SSD_STRIDE
    span = 8 * stride
    inner = SSD_HEADS * SSD_HEAD_DIM
    gw = inner // SSD_GROUPS
    gs = SSD_GROUPS * SSD_STATE
    nx = xs_ref.shape[1]
    tile_starts = [g * span + i for g in range(ck // span) for i in range(stride)]
    tile_rows = [pl.ds(start, 8, stride=stride) for start in tile_starts]
    ntile = len(tile_rows)

    @pl.when(pl.program_id(1) == 0)
    def _():
        prev_ref[...] = jnp.zeros_like(prev_ref)
        state_ref[...] = jnp.zeros_like(state_ref)

    first_row = lax.broadcasted_iota(jnp.int32, (8, LANES), 0) == 0
    lane = lax.broadcasted_iota(jnp.int32, (ck, LANES), 1)
    row = lax.broadcasted_iota(jnp.int32, (ck, LANES), 0)
    head0 = lane < SSD_HEAD_DIM
    tri_mask = _ssd_row_time(lane) <= _ssd_row_time(row)
    heads_per_group = SSD_HEADS // SSD_GROUPS

    def conv_silu(ch, c):
        lanes = slice(c * LANES, (c + 1) * LANES)
        src_ref, sc = (xs_ref, c) if c < nx else (bc_ref, c - nx)
        tiles = [src_ref[ch, sc, rows, :] for rows in tile_rows]

        def before(j):
            return prev_ref[j - 1, :, lanes] if ch == 0 else src_ref[ch - 1, sc, tile_rows[ntile - j], :]

        down = {-j: pltpu.roll(before(j), 1, 0) for j in range(1, SSD_CONV)}
        for n, tile in enumerate(tiles):
            if n % stride >= stride - (SSD_CONV - 1):
                down[n] = pltpu.roll(tile, 1, 0)
        wrapped = {}

        def back(n, k):
            if n % stride >= k:
                return tiles[n - k]
            if n - k not in wrapped:
                wrapped[n - k] = jnp.where(first_row, down[n - k], down[n - k + stride])
            return wrapped[n - k]

        out = []
        for n in range(ntile):
            acc = cb_ref[:, lanes]
            for k in range(SSD_CONV):
                acc = acc + back(n, k) * cw_ref[SSD_CONV - 1 - k:SSD_CONV - k, lanes]
            out.append(_silu(acc))
        if ch == z_ref.shape[0] - 1:
            for j in range(1, SSD_CONV):
                prev_ref[j - 1, :, lanes] = tiles[ntile - j]
        return jnp.concatenate(out, axis=0)

    for ch in range(z_ref.shape[0]):
        rows_ch = slice(ch * ck, (ch + 1) * ck)
        dt_raw = jnp.concatenate([dt_ref[pl.ds(ch * ck + start, 8, stride=stride), :] for start in tile_starts], axis=0)
        dt = _softplus(dt_raw + dtb_ref[...])
        dtt = _softplus(dtt_ref[:, rows_ch] + dtbt_ref[...])
        da = dt * (-jnp.exp(alog_ref[...]))
        dat = dtt * (-jnp.exp(alogt_ref[...]))
        a_cs = _dot_split_rhs(tri_ref[...], da, 3)
        a_cst = _dot_split_lhs(dat, trit_ref[...], 3)
        a_last = a_cs[ck - 1:ck, :]

        ea = jnp.exp(a_cs)
        dte = jnp.exp(a_last - a_cs)

        for g in range(SSD_GROUPS):
            cols = slice(g * gw, (g + 1) * gw)
            chunks = range(g * gw // LANES, (g + 1) * gw // LANES)
            xs = jnp.concatenate([conv_silu(ch, c) for c in chunks], axis=1)
            bg = conv_silu(ch, nx + g)
            cg16 = conv_silu(ch, nx + SSD_GROUPS + g).astype(BF16)
            e = e_ref[:, cols]
            ea_x = _dot_split_lhs(ea, e, 2)
            x_dt = xs * _dot_split_lhs(dt, e, 2)
            x_dt16 = x_dt.astype(BF16)
            x_end16 = (x_dt * _dot_split_lhs(dte, e, 2)).astype(BF16)

            cb = _dot_nt(cg16, bg.astype(BF16))
            y_pairs = []
            for rp in range(heads_per_group // 2):
                xp = x_dt16[:, rp * LANES:(rp + 1) * LANES]
                ys = []
                for hh in range(2):
                    r = g * heads_per_group + 2 * rp + hh
                    seg = a_cs[:, r:r + 1] - a_cst[r:r + 1, :]
                    lmat = cb * jnp.exp(jnp.where(tri_mask, seg, -jnp.inf))
                    ys.append(_dot(lmat.astype(BF16), xp))
                y_pairs.append(jnp.where(head0, ys[0], ys[1]))

            state = state_ref[:, cols]
            y_off = _dot(cg16, state.astype(BF16)) * ea_x
            contrib = _dot(bg.T.astype(BF16), x_end16)
            state_ref[:, cols] = state * ea_x[ck - 1:ck, :] + contrib

            y = jnp.concatenate(y_pairs, axis=1) + y_off + dexp_ref[:, cols] * xs
            for c in chunks:
                for n, rows in enumerate(tile_rows):
                    yout_ref[ch, c, rows, :] = y[8 * n:8 * n + 8, (c - chunks[0]) * LANES:(c - chunks[0] + 1) * LANES]
            y = jnp.concatenate([yout_ref[ch, c] for c in chunks], axis=1)
            y = y * jnp.concatenate([z_ref[ch, c] for c in chunks], axis=1)
            ms = jnp.mean(y * y, axis=-1, keepdims=True)
            o_ref[0, rows_ch, cols] = (y * lax.rsqrt(ms + EPS) * gn_ref[:, cols]).astype(o_ref.dtype)


def _ssd_branch(proj, dt, dtt, b, s, conv_w, conv_b, dt_bias, a_log, d_skip, ssd_norm):
    inner = SSD_HEADS * SSD_HEAD_DIM
    gs = SSD_GROUPS * SSD_STATE
    per = SSD_CHUNKS_PER_STEP
    nc = s // (SSD_CHUNK * per)
    nx, nbc = inner // LANES, 2 * gs // LANES
    chunked = lambda n, at: pl.BlockSpec((per, n, SSD_CHUNK, LANES), lambda bi, ci: (bi * nc + ci, at // n, 0, 0))
    idx = np.arange(SSD_CHUNK)
    when = _ssd_row_time(idx)
    tri = jnp.asarray((when[None, :] <= when[:, None]).astype(np.float32), BF16)
    trit = jnp.asarray((idx[:, None] <= when[None, :]).astype(np.float32), BF16)
    expand = np.zeros((LANES, inner), np.float32)
    expand[np.arange(inner) // SSD_HEAD_DIM, np.arange(inner)] = 1.0
    expand = jnp.asarray(expand, BF16)

    pad = LANES - SSD_HEADS
    row = lambda v: v.reshape(1, -1)
    dtb = jnp.pad(row(dt_bias), ((0, 0), (0, pad)))
    alog = jnp.pad(row(a_log), ((0, 0), (0, pad)))
    dexp = row(jnp.repeat(d_skip, SSD_HEAD_DIM))

    const = lambda shape: pl.BlockSpec(shape, lambda bi, ci: (0,) * len(shape))
    return pl.pallas_call(
        _ssd_kernel, grid=(b, nc),
        in_specs=[
            chunked(nx, 0),
            chunked(nx, nx),
            chunked(nbc, 2 * nx),
            pl.BlockSpec((per * SSD_CHUNK, LANES), lambda bi, ci: (bi * nc + ci, 0)),
            pl.BlockSpec((SSD_HEADS, per * SSD_CHUNK), lambda bi, ci: (0, bi * nc + ci)),
            const((SSD_CONV, inner + 2 * gs)), const((1, inner + 2 * gs)),
            const((1, LANES)), const((SSD_HEADS, 1)), const((1, LANES)), const((SSD_HEADS, 1)),
            const((1, inner)), const((1, inner)),
            const((SSD_CHUNK, SSD_CHUNK)), const((SSD_CHUNK, SSD_CHUNK)), const((LANES, inner)),
        ],
        out_specs=pl.BlockSpec((1, per * SSD_CHUNK, inner), lambda bi, ci: (bi, ci, 0)),
        out_shape=jax.ShapeDtypeStruct((b, s, inner), BF16),
        scratch_shapes=[pltpu.VMEM((SSD_CONV - 1, 8, inner + 2 * gs), F32),
                        pltpu.VMEM((SSD_STATE, inner), F32),
                        pltpu.VMEM((per, inner // LANES, SSD_CHUNK, LANES), F32)],
        compiler_params=pltpu.CompilerParams(
            dimension_semantics=("arbitrary", "arbitrary"), vmem_limit_bytes=VMEM_LIMIT),
        name="ssd_branch",
    )(proj, proj, proj, dt, dtt, conv_w, row(conv_b),
      dtb, dt_bias.reshape(-1, 1), alog, a_log.reshape(-1, 1), dexp, row(ssd_norm),
      tri, trit, expand)


def _mem_attn_kernel(q_ref, kv_ref, o_ref):
    width = MEM_HEADS * MEM_HEAD_DIM
    scale = MEM_HEAD_DIM ** -0.5
    head_cols = [slice(h * MEM_HEAD_DIM, (h + 1) * MEM_HEAD_DIM) for h in range(MEM_HEADS)]
    scores = [_dot_nt(q_ref[0, :, cols], kv_ref[0, :, cols]) * scale for cols in head_cols]
    probs = []
    for sc in scores:
        p = jnp.exp(sc - jnp.max(sc, axis=-1, keepdims=True))
        probs.append((p / jnp.sum(p, axis=-1, keepdims=True)).astype(BF16))
    for h, cols in enumerate(head_cols):
        v = kv_ref[0, :, width + h * MEM_HEAD_DIM:width + (h + 1) * MEM_HEAD_DIM]
        o_ref[0, :, cols] = _dot(probs[h], v).astype(o_ref.dtype)


def _mem_attention(qkv, kv, b, s, tq):
    width = MEM_HEADS * MEM_HEAD_DIM
    mlen = kv.shape[1]
    return pl.pallas_call(
        _mem_attn_kernel, grid=(b, s // tq),
        in_specs=[pl.BlockSpec((1, tq, width), lambda bi, qi: (bi, qi, 3)),
                  pl.BlockSpec((1, mlen, 2 * width), lambda bi, qi: (bi, 0, 0))],
        out_specs=pl.BlockSpec((1, tq, width), lambda bi, qi: (bi, qi, 0)),
        out_shape=jax.ShapeDtypeStruct((b, s, width), BF16),
        compiler_params=pltpu.CompilerParams(
            dimension_semantics=("arbitrary", "arbitrary"), vmem_limit_bytes=VMEM_LIMIT),
        name="mem_attention",
    )(qkv, kv)


def _merge_mlp_kernel(x_ref, ysb_ref, yssd_ref, ymem_ref, g0_ref, g1_ref, g2_ref,
                      wsb_ref, wssd_ref, wmem_ref, wo_ref, gmix_ref,
                      gpre_ref, wup_ref, wdown_ref, gpost_ref, o_ref, *, chunk):
    def gate(g_ref):
        logits = jnp.concatenate([jnp.concatenate([g_ref[r, c] for c in range(g_ref.shape[1])], axis=1)
                                  for r in range(g_ref.shape[0])], axis=0)
        return _sigmoid(logits)

    merged = (gate(g0_ref) * _dot(ysb_ref[...], wsb_ref[...])
              + gate(g1_ref) * _dot(yssd_ref[...], wssd_ref[...])
              + gate(g2_ref) * _dot(ymem_ref[...], wmem_ref[...]))
    mix = _dot(merged.astype(BF16), wo_ref[...])
    h = x_ref[...] + _rms_rows(mix, gmix_ref[...])

    u = _rms_rows(h, gpre_ref[...]).astype(BF16)
    ff = None
    for c in range(wup_ref.shape[1] // chunk):
        hid = _dot(u, wup_ref[:, c * chunk:(c + 1) * chunk])
        act = jnp.square(jnp.maximum(hid, 0.0)).astype(BF16)
        t = _dot(act, wdown_ref[c * chunk:(c + 1) * chunk, :])
        ff = t if ff is None else ff + t
    o_ref[...] = h + _rms_rows(ff, gpost_ref[...])


def _merge_mlp(x, y_sb, y_ssd, y_mem, proj, w_sb, w_ssd, w_mem, w_o, g_mix,
               g_pre, w_up, w_down, g_post, tm):
    m, d = x.shape
    inner = y_ssd.shape[1]
    dc = d // LANES
    gate0 = proj.shape[1] // dc - N_GATES
    tile = lambda w: pl.BlockSpec((tm, w), lambda i: (i, 0))
    gate = lambda k: pl.BlockSpec((tm // LANES, dc, LANES, LANES), lambda i: (i, gate0 + k, 0, 0))
    full = lambda a: pl.BlockSpec(a.shape, lambda i: (0, 0), pipeline_mode=pl.Buffered(1))
    return pl.pallas_call(
        functools.partial(_merge_mlp_kernel, chunk=1024), grid=(m // tm,),
        in_specs=[tile(d), tile(d), tile(inner), tile(d), gate(0), gate(1), gate(2),
                  full(w_sb), full(w_ssd), full(w_mem), full(w_o), full(g_mix),
                  full(g_pre), full(w_up), full(w_down), full(g_post)],
        out_specs=tile(d),
        out_shape=jax.ShapeDtypeStruct((m, d), F32),
        compiler_params=pltpu.CompilerParams(
            dimension_semantics=("arbitrary",), vmem_limit_bytes=VMEM_LIMIT),
        name="merge_mlp",
    )(x, y_sb, y_ssd, y_mem, proj, proj, proj, w_sb, w_ssd, w_mem, w_o, g_mix,
      g_pre, w_up, w_down, g_post)


def _layer(h, mem, norm_mix_pre, w_in, conv_w, conv_b, dt_bias, a_log, d_skip, ssd_norm,
           norm_mem, w_mem_kv, w_sb_out, w_ssd_out, w_mem_out, w_o, norm_mix_post,
           norm_mlp_pre, w_up, w_down, norm_mlp_post):
    b, s, d = h.shape
    m = b * s
    row = lambda v: v.reshape(1, -1)
    x2 = h.reshape(m, d)

    sb_w = 3 * SB_HEADS * SB_HEAD_DIM
    inner = SSD_HEADS * SSD_HEAD_DIM
    conv_dim = inner + 2 * SSD_GROUPS * SSD_STATE
    o_z, o_xbc, o_dt = sb_w, sb_w + inner, sb_w + inner + conv_dim
    o_memq = o_dt + SSD_HEADS
    o_gate = o_memq + MEM_HEADS * MEM_HEAD_DIM

    tn = 1024
    w_t = jnp.swapaxes(w_in, 0, 1).astype(BF16)
    w_dt_t = jnp.pad(w_t[o_dt:o_memq], ((0, LANES - SSD_HEADS), (0, 0)))
    a_tiles = list(range(sb_w // tn)) + list(range(o_dt // tn, (o_dt + o_gate - o_memq) // tn))
    z_tiles = list(range(o_z // tn, o_xbc // tn))
    qkv, proj, dt, dtt = _in_proj(x2, row(norm_mix_pre), w_t, (o_dt, o_memq), a_tiles, z_tiles, w_dt_t,
                                  SSD_HEADS, 2048, tn)
    qkv = qkv.reshape(b, s, -1)

    y_sb = _sb_attention(qkv, b, s, 1024)
    y_ssd = _ssd_branch(proj, dt, dtt, b, s, conv_w, conv_b, dt_bias, a_log, d_skip, ssd_norm)

    mlen = mem.shape[1]
    kv = _norm_proj(mem.reshape(b * mlen, d), row(norm_mem), w_mem_kv.astype(BF16), BF16, b * mlen, 1024)
    y_mem = _mem_attention(qkv, kv.reshape(b, mlen, -1), b, s, 1024)

    out = _merge_mlp(x2, y_sb.reshape(m, -1), y_ssd.reshape(m, -1), y_mem.reshape(m, -1), proj,
                     w_sb_out.astype(BF16), w_ssd_out.astype(BF16), w_mem_out.astype(BF16),
                     w_o.astype(BF16), row(norm_mix_post),
                     row(norm_mlp_pre), w_up.astype(BF16), w_down.astype(BF16), row(norm_mlp_post), 256)
    return out.reshape(b, s, d)


def kernel(x, mem, norm_mix_pre, w_in, conv_w, conv_b, dt_bias, a_log, d_skip, ssd_norm, norm_mem, w_mem_kv, w_sb_out, w_ssd_out, w_mem_out, w_o, norm_mix_post, norm_mlp_pre, w_up, w_down, norm_mlp_post):
    h = x
    for layer in range(w_in.shape[0]):
        h = _layer(h, mem, norm_mix_pre[layer], w_in[layer], conv_w[layer], conv_b[layer],
                   dt_bias[layer], a_log[layer], d_skip[layer], ssd_norm[layer], norm_mem[layer],
                   w_mem_kv[layer], w_sb_out[layer], w_ssd_out[layer], w_mem_out[layer], w_o[layer],
                   norm_mix_post[layer], norm_mlp_pre[layer], w_up[layer], w_down[layer],
                   norm_mlp_post[layer])
    return h
```

```python
import functools

import jax
import jax.numpy as jnp
import numpy as np
from jax import lax
from jax.experimental import pallas as pl
from jax.experimental.pallas import tpu as pltpu

F32 = jnp.float32
BF16 = jnp.bfloat16

EPS = 1e-6
LOG2E = 1.4426950408889634
LANES = 128
SB_HEADS = 16
SB_HEAD_DIM = 64
SB_BLOCK = 128
SSD_HEADS = 32
SSD_HEAD_DIM = 64
SSD_GROUPS = 4
SSD_STATE = 128
SSD_CHUNK = 128
SSD_CONV = 4
MEM_HEADS = 4
MEM_HEAD_DIM = 256
N_GATES = 3
VMEM_LIMIT = 56 * 1024 * 1024

SB_EXP_UNDERFLOW = 110.0
SB_NO_KEYS = 1e30


SSD_STRIDE = 4
SSD_CHUNKS_PER_STEP = 4


def _ssd_row_time(pos):
    span = 8 * SSD_STRIDE
    return (pos & -span) + (pos & 7) * SSD_STRIDE + ((pos >> 3) & (SSD_STRIDE - 1))


def _dot(a, b):
    return jnp.dot(a, b, preferred_element_type=F32)


def _dot_nt(a, b):
    return lax.dot_general(a, b, (((1,), (1,)), ((), ())), preferred_element_type=F32)


def _split_bf16(x, parts):
    out = []
    rem = x
    for _ in range(parts):
        p = rem.astype(BF16)
        out.append(p)
        rem = rem - p.astype(F32)
    return out


def _dot_split_lhs(x, m, parts):
    acc = None
    for p in _split_bf16(x, parts):
        t = _dot(p, m)
        acc = t if acc is None else acc + t
    return acc


def _dot_split_rhs(m, x, parts):
    acc = None
    for p in _split_bf16(x, parts):
        t = _dot(m, p)
        acc = t if acc is None else acc + t
    return acc


def _rms_rows(x, gain):
    ms = jnp.mean(x * x, axis=-1, keepdims=True)
    return x * lax.rsqrt(ms + EPS) * gain


def _softplus(x):
    return jnp.maximum(x, 0.0) + jnp.log1p(jnp.exp(-jnp.abs(x)))


def _sigmoid(x):
    return 1.0 / (1.0 + jnp.exp2(x * -LOG2E))


def _silu(x):
    h = 0.5 * x
    return h + h * jnp.tanh(h)


def _norm_proj_kernel(x_ref, g_ref, w_ref, o_ref, u_ref):
    @pl.when(pl.program_id(1) == 0)
    def _():
        u_ref[...] = _rms_rows(x_ref[...], g_ref[...]).astype(BF16)

    o_ref[...] = _dot(u_ref[...], w_ref[...]).astype(o_ref.dtype)


def _norm_proj(x, gain, w, out_dtype, tm, tn):
    m, d = x.shape
    n = w.shape[1]
    return pl.pallas_call(
        _norm_proj_kernel, grid=(m // tm, n // tn),
        in_specs=[pl.BlockSpec((tm, d), lambda i, j: (i, 0)),
                  pl.BlockSpec((1, d), lambda i, j: (0, 0)),
                  pl.BlockSpec((d, tn), lambda i, j: (0, j))],
        out_specs=pl.BlockSpec((tm, tn), lambda i, j: (i, j)),
        out_shape=jax.ShapeDtypeStruct((m, n), out_dtype),
        scratch_shapes=[pltpu.VMEM((tm, d), BF16)],
        compiler_params=pltpu.CompilerParams(dimension_semantics=("arbitrary", "arbitrary"),
                                             vmem_limit_bytes=VMEM_LIMIT),
        name="norm_proj",
    )(x, gain, w)


def _in_proj_kernel(x_ref, g_ref, wt_ref, wdtt_ref, oa_ref, oc_ref, dt_ref, dtt_ref, u_ref, *,
                    nt, a_tiles, silu_tiles, nh):
    j = pl.program_id(1)

    @pl.when(j == 0)
    def _():
        u = _rms_rows(x_ref[...], g_ref[...]).astype(BF16)
        u_ref[...] = u
        dt_ref[...] = _dot_nt(u, wdtt_ref[...])
        dtt_ref[...] = _dot_nt(wdtt_ref[0:nh, :], u)

    tile_in = lambda tiles: functools.reduce(jnp.logical_or, [j == t for t in tiles])
    plain_tiles = [t for t in range(nt) if t not in a_tiles and t not in silu_tiles]

    def store_chunked(res):
        for r in range(oc_ref.shape[0]):
            for c in range(oc_ref.shape[1]):
                oc_ref[r, c] = res[r * LANES:(r + 1) * LANES, c * LANES:(c + 1) * LANES]

    @pl.when(tile_in(a_tiles))
    def _():
        oa_ref[...] = _dot_nt(u_ref[...], wt_ref[...]).astype(oa_ref.dtype)

    @pl.when(tile_in(silu_tiles))
    def _():
        store_chunked(_silu(_dot_nt(u_ref[...], wt_ref[...])))

    @pl.when(tile_in(plain_tiles))
    def _():
        store_chunked(_dot_nt(u_ref[...], wt_ref[...]))


def _in_proj(x, gain, w_t, skip, a_tiles, silu_tiles, w_dt_t, nh, tm, tn):
    m, d = x.shape
    nt = (w_t.shape[0] - (skip[1] - skip[0])) // tn
    c_tiles = [t for t in range(nt) if t not in a_tiles]
    skip_tile, skip_rows = skip[0] // tn, skip[1] - skip[0]

    order, na = [], 0
    for k, t in enumerate(c_tiles):
        order.append(t)
        if (k + 1) * len(a_tiles) // len(c_tiles) > na:
            order.append(a_tiles[na])
            na += 1
    where = lambda tiles: [p for p, t in enumerate(order) if t in tiles]

    def weight_rows(j):
        tile = sum((j == p).astype(jnp.int32) * t for p, t in enumerate(order))
        return (tile * (tn // skip_rows) + (tile >= skip_tile).astype(jnp.int32)) * skip_rows

    def rank(positions, j):
        return jnp.maximum(sum((j >= p).astype(jnp.int32) for p in positions) - 1, 0)

    return pl.pallas_call(
        functools.partial(_in_proj_kernel, nt=nt, a_tiles=tuple(where(a_tiles)),
                          silu_tiles=tuple(where(silu_tiles)), nh=nh),
        grid=(m // tm, nt),
        in_specs=[pl.BlockSpec((tm, d), lambda i, j: (i, 0), pipeline_mode=pl.Buffered(1)),
                  pl.BlockSpec((1, d), lambda i, j: (0, 0)),
                  pl.BlockSpec((pl.Element(tn), pl.Element(d)), lambda i, j: (weight_rows(j), 0)),
                  pl.BlockSpec((LANES, d), lambda i, j: (0, 0))],
        out_specs=[pl.BlockSpec((tm, tn), lambda i, j: (i, rank(where(a_tiles), j))),
                   pl.BlockSpec((tm // LANES, tn // LANES, LANES, LANES),
                                lambda i, j: (i, rank(where(c_tiles), j), 0, 0)),
                   pl.BlockSpec((tm, LANES), lambda i, j: (i, 0)),
                   pl.BlockSpec((nh, tm), lambda i, j: (0, i))],
        out_shape=[jax.ShapeDtypeStruct((m, len(a_tiles) * tn), BF16),
                   jax.ShapeDtypeStruct((m // LANES, len(c_tiles) * tn // LANES, LANES, LANES), F32),
                   jax.ShapeDtypeStruct((m, LANES), F32),
                   jax.ShapeDtypeStruct((nh, m), F32)],
        scratch_shapes=[pltpu.VMEM((tm, d), BF16)],
        compiler_params=pltpu.CompilerParams(dimension_semantics=("arbitrary", "arbitrary"),
                                             vmem_limit_bytes=VMEM_LIMIT),
        name="in_proj",
    )(x, gain, w_t, w_dt_t)


def _sb_kernel(q_ref, k_ref, v_ref, mm_ref, o_ref, qs_ref, kc_ref, vc_ref, acc_ref, c_ref, *, nsub):
    qi = pl.program_id(2)
    blk = SB_BLOCK
    nblk = v_ref.shape[1] // blk
    head0 = lax.broadcasted_iota(jnp.int32, (blk, LANES), 1) < SB_HEAD_DIM
    key = jnp.bitwise_and(lax.broadcasted_iota(jnp.int32, (blk, 2 * blk), 1), blk - 1)
    causal = key < lax.broadcasted_iota(jnp.int32, (blk, 2 * blk), 0)
    scale = SB_HEAD_DIM ** -0.5

    keep0 = jnp.where(head0, 1.0, 0.0).astype(BF16)
    keep1 = jnp.where(head0, 0.0, 1.0).astype(BF16)

    def stack_heads(x):
        return jnp.concatenate([x * keep0, x * keep1], axis=0)

    @pl.when(qi == 0)
    def _():
        def fill(j, carry):
            off = pl.multiple_of(j * blk, blk)
            kc_ref[j] = stack_heads(k_ref[0, pl.ds(off, blk), :])
            vc_ref[j] = stack_heads(v_ref[0, pl.ds(off, blk), :])
            return carry
        lax.fori_loop(0, nblk, fill, 0)

    qs_ref[...] = (q_ref[0].astype(F32) * scale).astype(BF16)

    def visit(s, spans, first):
        units = [(t, i) for t in range(nsub) for i in range(len(spans))]
        dist = {u: spans[u[1]][0] for u in units}
        rows = {u: slice(spans[u[1]][1], spans[u[1]][2]) for u in units}
        diag = {u: first and dist[u] == 0 for u in units}
        js = {u: qi * nsub + u[0] - (s + dist[u]) for u in units}
        jcs = {u: js[u] if diag[u] else jnp.maximum(js[u], 0) for u in units}
        zs = {u: _dot_nt(qs_ref[u[0] * blk + rows[u].start:u[0] * blk + rows[u].stop, :], kc_ref[jcs[u]])
              for u in units}
        log_betas, sums = {}, {}
        for u in units:
            z = zs[u]
            sp = jnp.log(1.0 + jnp.exp2(jnp.abs(z) * -LOG2E))
            log_beta = jnp.minimum(z, 0.0) - sp
            log_keep = log_beta - z
            if diag[u]:
                log_keep = jnp.where(causal, log_keep, 0.0)
            log_betas[u] = log_beta
            hi, lo = _split_bf16(log_keep, 2)
            sums[u] = [_dot(jnp.concatenate([hi[:, h * blk:(h + 1) * blk], lo[:, h * blk:(h + 1) * blk]], axis=1),
                            mm_ref[...]) for h in range(2)]

        def put(full, part, r):
            pieces = ([full[:r.start]] if r.start > 0 else []) + [part]
            pieces += [full[r.stop:]] if r.stop < full.shape[0] else []
            return pieces[0] if len(pieces) == 1 else jnp.concatenate(pieces, axis=0)

        cmax = None
        for t in range(nsub):
            c = None if first else c_ref[t]
            acc = None if first else acc_ref[t]
            for i in range(len(spans)):
                u = (t, i)
                r0, r1 = sums[u]
                later = jnp.concatenate([r0[:, :blk], r1[:, :blk]], axis=1)
                total = jnp.concatenate([r0[:, blk:], r1[:, blk:]], axis=1)
                if diag[u]:
                    w = jnp.where(causal, jnp.exp(log_betas[u] + later), 0.0)
                    c = total
                    acc = _dot(w.astype(BF16), vc_ref[jcs[u]])
                else:
                    c_rows = jnp.where(js[u] >= 0, c[rows[u]], -SB_NO_KEYS)
                    w = jnp.exp(log_betas[u] + later + c_rows)
                    c = put(c, c_rows + total, rows[u])
                    acc = put(acc, acc[rows[u]] + _dot(w.astype(BF16), vc_ref[jcs[u]]), rows[u])
            acc_ref[t] = acc
            c_ref[t] = c
            cmax = c if cmax is None else jnp.maximum(cmax, c)
        return jnp.max(cmax)

    half = blk // 2
    last = qi * nsub + nsub - 1
    alive = lambda cm: cm > -SB_EXP_UNDERFLOW

    cm = visit(0, [(0, 0, blk), (1, 0, blk), (2, 0, half)], True)
    cm_low = jnp.max(functools.reduce(jnp.maximum, [c_ref[t, half:, :] for t in range(nsub)]))
    cm = lax.cond(alive(cm_low), lambda: visit(2, [(0, half, blk)], False), lambda: cm)

    def cond(st):
        return jnp.logical_and(st[0] <= last, alive(st[1]))

    def body(st):
        return st[0] + 1, visit(st[0], [(0, 0, blk)], False)

    lax.while_loop(cond, body, (jnp.int32(3), cm))
    for t in range(nsub):
        o_ref[0, t * blk:(t + 1) * blk, :] = acc_ref[t].astype(o_ref.dtype)


def _sb_attention(qkv, b, s, tq):
    pairs = SB_HEADS * SB_HEAD_DIM // LANES
    nsub = tq // SB_BLOCK
    idx = np.arange(SB_BLOCK)
    later = (idx[:, None] > idx[None, :]).astype(np.float32)
    m = np.concatenate([later, np.ones((SB_BLOCK, SB_BLOCK), np.float32)], axis=1)
    mm = jnp.asarray(np.concatenate([m, m], axis=0), BF16)
    return pl.pallas_call(
        functools.partial(_sb_kernel, nsub=nsub),
        grid=(b, pairs, s // tq),
        in_specs=[
            pl.BlockSpec((1, tq, LANES), lambda bi, hp, qi: (bi, qi, hp)),
            pl.BlockSpec((1, s, LANES), lambda bi, hp, qi: (bi, 0, pairs + hp)),
            pl.BlockSpec((1, s, LANES), lambda bi, hp, qi: (bi, 0, 2 * pairs + hp)),
            pl.BlockSpec((2 * SB_BLOCK, 2 * SB_BLOCK), lambda bi, hp, qi: (0, 0)),
        ],
        out_specs=pl.BlockSpec((1, tq, LANES), lambda bi, hp, qi: (bi, qi, hp)),
        out_shape=jax.ShapeDtypeStruct((b, s, SB_HEADS * SB_HEAD_DIM), BF16),
        scratch_shapes=[pltpu.VMEM((tq, LANES), BF16),
                        pltpu.VMEM((s // SB_BLOCK, 2 * SB_BLOCK, LANES), BF16),
                        pltpu.VMEM((s // SB_BLOCK, 2 * SB_BLOCK, LANES), BF16),
                        pltpu.VMEM((nsub, SB_BLOCK, LANES), F32),
                        pltpu.VMEM((nsub, SB_BLOCK, 2 * SB_BLOCK), F32)],
        compiler_params=pltpu.CompilerParams(
            dimension_semantics=("arbitrary", "arbitrary", "arbitrary"), vmem_limit_bytes=VMEM_LIMIT),
        name="sb_attention",
    )(qkv, qkv, qkv, mm)


def _ssd_kernel(z_ref, xs_ref, bc_ref, dt_ref, dtt_ref, cw_ref, cb_ref,
                dtb_ref, dtbt_ref, alog_ref, alogt_ref, dexp_ref, gn_ref,
                tri_ref, trit_ref, e_ref,
                o_ref, prev_ref, state_ref, yout_ref):
    ck = SSD_CHUNK
    stride = SSD_STRIDE
    span = 8 * stride
    inner = SSD_HEADS * SSD_HEAD_DIM
    gw = inner // SSD_GROUPS
    gs = SSD_GROUPS * SSD_STATE
    nx = xs_ref.shape[1]
    tile_starts = [g * span + i for g in range(ck // span) for i in range(stride)]
    tile_rows = [pl.ds(start, 8, stride=stride) for start in tile_starts]
    ntile = len(tile_rows)

    @pl.when(pl.program_id(1) == 0)
    def _():
        prev_ref[...] = jnp.zeros_like(prev_ref)
        state_ref[...] = jnp.zeros_like(state_ref)

    first_row = lax.broadcasted_iota(jnp.int32, (8, LANES), 0) == 0
    lane = lax.broadcasted_iota(jnp.int32, (ck, LANES), 1)
    row = lax.broadcasted_iota(jnp.int32, (ck, LANES), 0)
    head0 = lane < SSD_HEAD_DIM
    tri_mask = _ssd_row_time(lane) <= _ssd_row_time(row)
    heads_per_group = SSD_HEADS // SSD_GROUPS

    def conv_silu(ch, c):
        lanes = slice(c * LANES, (c + 1) * LANES)
        src_ref, sc = (xs_ref, c) if c < nx else (bc_ref, c - nx)
        tiles = [src_ref[ch, sc, rows, :] for rows in tile_rows]

        def before(j):
            return prev_ref[j - 1, :, lanes] if ch == 0 else src_ref[ch - 1, sc, tile_rows[ntile - j], :]

        down = {-j: pltpu.roll(before(j), 1, 0) for j in range(1, SSD_CONV)}
        for n, tile in enumerate(tiles):
            if n % stride >= stride - (SSD_CONV - 1):
                down[n] = pltpu.roll(tile, 1, 0)
        wrapped = {}

        def back(n, k):
            if n % stride >= k:
                return tiles[n - k]
            if n - k not in wrapped:
                wrapped[n - k] = jnp.where(first_row, down[n - k], down[n - k + stride])
            return wrapped[n - k]

        out = []
        for n in range(ntile):
            acc = cb_ref[:, lanes]
            for k in range(SSD_CONV):
                acc = acc + back(n, k) * cw_ref[SSD_CONV - 1 - k:SSD_CONV - k, lanes]
            out.append(_silu(acc))
        if ch == z_ref.shape[0] - 1:
            for j in range(1, SSD_CONV):
                prev_ref[j - 1, :, lanes] = tiles[ntile - j]
        return jnp.concatenate(out, axis=0)

    for ch in range(z_ref.shape[0]):
        rows_ch = slice(ch * ck, (ch + 1) * ck)
        dt_raw = jnp.concatenate([dt_ref[pl.ds(ch * ck + start, 8, stride=stride), :] for start in tile_starts], axis=0)
        dt = _softplus(dt_raw + dtb_ref[...])
        dtt = _softplus(dtt_ref[:, rows_ch] + dtbt_ref[...])
        da = dt * (-jnp.exp(alog_ref[...]))
        dat = dtt * (-jnp.exp(alogt_ref[...]))
        a_cs = _dot_split_rhs(tri_ref[...], da, 3)
        a_cst = _dot_split_lhs(dat, trit_ref[...], 3)
        a_last = a_cs[ck - 1:ck, :]

        ea = jnp.exp(a_cs)
        dte = jnp.exp(a_last - a_cs)

        for g in range(SSD_GROUPS):
            cols = slice(g * gw, (g + 1) * gw)
            chunks = range(g * gw // LANES, (g + 1) * gw // LANES)
            xs = jnp.concatenate([conv_silu(ch, c) for c in chunks], axis=1)
            bg = conv_silu(ch, nx + g)
            cg16 = conv_silu(ch, nx + SSD_GROUPS + g).astype(BF16)
            e = e_ref[:, cols]
            ea_x = _dot_split_lhs(ea, e, 2)
            x_dt = xs * _dot_split_lhs(dt, e, 2)
            x_dt16 = x_dt.astype(BF16)
            x_end16 = (x_dt * _dot_split_lhs(dte, e, 2)).astype(BF16)

            cb = _dot_nt(cg16, bg.astype(BF16))
            y_pairs = []
            for rp in range(heads_per_group // 2):
                xp = x_dt16[:, rp * LANES:(rp + 1) * LANES]
                ys = []
                for hh in range(2):
                    r = g * heads_per_group + 2 * rp + hh
                    seg = a_cs[:, r:r + 1] - a_cst[r:r + 1, :]
                    lmat = cb * jnp.exp(jnp.where(tri_mask, seg, -jnp.inf))
                    ys.append(_dot(lmat.astype(BF16), xp))
                y_pairs.append(jnp.where(head0, ys[0], ys[1]))

            state = state_ref[:, cols]
            y_off = _dot(cg16, state.astype(BF16)) * ea_x
            contrib = _dot(bg.T.astype(BF16), x_end16)
            state_ref[:, cols] = state * ea_x[ck - 1:ck, :] + contrib

            y = jnp.concatenate(y_pairs, axis=1) + y_off + dexp_ref[:, cols] * xs
            for c in chunks:
                for n, rows in enumerate(tile_rows):
                    yout_ref[ch, c, rows, :] = y[8 * n:8 * n + 8, (c - chunks[0]) * LANES:(c - chunks[0] + 1) * LANES]
            y = jnp.concatenate([yout_ref[ch, c] for c in chunks], axis=1)
            y = y * jnp.concatenate([z_ref[ch, c] for c in chunks], axis=1)
            ms = jnp.mean(y * y, axis=-1, keepdims=True)
            o_ref[0, rows_ch, cols] = (y * lax.rsqrt(ms + EPS) * gn_ref[:, cols]).astype(o_ref.dtype)


def _ssd_branch(proj, dt, dtt, b, s, conv_w, conv_b, dt_bias, a_log, d_skip, ssd_norm):
    inner = SSD_HEADS * SSD_HEAD_DIM
    gs = SSD_GROUPS * SSD_STATE
    per = SSD_CHUNKS_PER_STEP
    nc = s // (SSD_CHUNK * per)
    nx, nbc = inner // LANES, 2 * gs // LANES
    chunked = lambda n, at: pl.BlockSpec((per, n, SSD_CHUNK, LANES), lambda bi, ci: (bi * nc + ci, at // n, 0, 0))
    idx = np.arange(SSD_CHUNK)
    when = _ssd_row_time(idx)
    tri = jnp.asarray((when[None, :] <= when[:, None]).astype(np.float32), BF16)
    trit = jnp.asarray((idx[:, None] <= when[None, :]).astype(np.float32), BF16)
    expand = np.zeros((LANES, inner), np.float32)
    expand[np.arange(inner) // SSD_HEAD_DIM, np.arange(inner)] = 1.0
    expand = jnp.asarray(expand, BF16)

    pad = LANES - SSD_HEADS
    row = lambda v: v.reshape(1, -1)
    dtb = jnp.pad(row(dt_bias), ((0, 0), (0, pad)))
    alog = jnp.pad(row(a_log), ((0, 0), (0, pad)))
    dexp = row(jnp.repeat(d_skip, SSD_HEAD_DIM))

    const = lambda shape: pl.BlockSpec(shape, lambda bi, ci: (0,) * len(shape))
    return pl.pallas_call(
        _ssd_kernel, grid=(b, nc),
        in_specs=[
            chunked(nx, 0),
            chunked(nx, nx),
            chunked(nbc, 2 * nx),
            pl.BlockSpec((per * SSD_CHUNK, LANES), lambda bi, ci: (bi * nc + ci, 0)),
            pl.BlockSpec((SSD_HEADS, per * SSD_CHUNK), lambda bi, ci: (0, bi * nc + ci)),
            const((SSD_CONV, inner + 2 * gs)), const((1, inner + 2 * gs)),
            const((1, LANES)), const((SSD_HEADS, 1)), const((1, LANES)), const((SSD_HEADS, 1)),
            const((1, inner)), const((1, inner)),
            const((SSD_CHUNK, SSD_CHUNK)), const((SSD_CHUNK, SSD_CHUNK)), const((LANES, inner)),
        ],
        out_specs=pl.BlockSpec((1, per * SSD_CHUNK, inner), lambda bi, ci: (bi, ci, 0)),
        out_shape=jax.ShapeDtypeStruct((b, s, inner), BF16),
        scratch_shapes=[pltpu.VMEM((SSD_CONV - 1, 8, inner + 2 * gs), F32),
                        pltpu.VMEM((SSD_STATE, inner), F32),
                        pltpu.VMEM((per, inner // LANES, SSD_CHUNK, LANES), F32)],
        compiler_params=pltpu.CompilerParams(
            dimension_semantics=("arbitrary", "arbitrary"), vmem_limit_bytes=VMEM_LIMIT),
        name="ssd_branch",
    )(proj, proj, proj, dt, dtt, conv_w, row(conv_b),
      dtb, dt_bias.reshape(-1, 1), alog, a_log.reshape(-1, 1), dexp, row(ssd_norm),
      tri, trit, expand)


def _mem_attn_kernel(q_ref, kv_ref, o_ref):
    width = MEM_HEADS * MEM_HEAD_DIM
    scale = MEM_HEAD_DIM ** -0.5
    head_cols = [slice(h * MEM_HEAD_DIM, (h + 1) * MEM_HEAD_DIM) for h in range(MEM_HEADS)]
    scores = [_dot_nt(q_ref[0, :, cols], kv_ref[0, :, cols]) * scale for cols in head_cols]
    probs = []
    for sc in scores:
        p = jnp.exp(sc - jnp.max(sc, axis=-1, keepdims=True))
        probs.append((p / jnp.sum(p, axis=-1, keepdims=True)).astype(BF16))
    for h, cols in enumerate(head_cols):
        v = kv_ref[0, :, width + h * MEM_HEAD_DIM:width + (h + 1) * MEM_HEAD_DIM]
        o_ref[0, :, cols] = _dot(probs[h], v).astype(o_ref.dtype)


def _mem_attention(qkv, kv, b, s, tq):
    width = MEM_HEADS * MEM_HEAD_DIM
    mlen = kv.shape[1]
    return pl.pallas_call(
        _mem_attn_kernel, grid=(b, s // tq),
        in_specs=[pl.BlockSpec((1, tq, width), lambda bi, qi: (bi, qi, 3)),
                  pl.BlockSpec((1, mlen, 2 * width), lambda bi, qi: (bi, 0, 0))],
        out_specs=pl.BlockSpec((1, tq, width), lambda bi, qi: (bi, qi, 0)),
        out_shape=jax.ShapeDtypeStruct((b, s, width), BF16),
        compiler_params=pltpu.CompilerParams(
            dimension_semantics=("arbitrary", "arbitrary"), vmem_limit_bytes=VMEM_LIMIT),
        name="mem_attention",
    )(qkv, kv)


def _merge_mlp_kernel(x_ref, ysb_ref, yssd_ref, ymem_ref, g0_ref, g1_ref, g2_ref,
                      wsb_ref, wssd_ref, wmem_ref, wo_ref, gmix_ref,
                      gpre_ref, wup_ref, wdown_ref, gpost_ref, o_ref, *, chunk):
    def gate(g_ref):
        logits = jnp.concatenate([jnp.concatenate([g_ref[r, c] for c in range(g_ref.shape[1])], axis=1)
                                  for r in range(g_ref.shape[0])], axis=0)
        return _sigmoid(logits)

    merged = (gate(g0_ref) * _dot(ysb_ref[...], wsb_ref[...])
              + gate(g1_ref) * _dot(yssd_ref[...], wssd_ref[...])
              + gate(g2_ref) * _dot(ymem_ref[...], wmem_ref[...]))
    mix = _dot(merged.astype(BF16), wo_ref[...])
    h = x_ref[...] + _rms_rows(mix, gmix_ref[...])

    u = _rms_rows(h, gpre_ref[...]).astype(BF16)
    ff = None
    for c in range(wup_ref.shape[1] // chunk):
        hid = _dot(u, wup_ref[:, c * chunk:(c + 1) * chunk])
        act = jnp.square(jnp.maximum(hid, 0.0)).astype(BF16)
        t = _dot(act, wdown_ref[c * chunk:(c + 1) * chunk, :])
        ff = t if ff is None else ff + t
    o_ref[...] = h + _rms_rows(ff, gpost_ref[...])


def _merge_mlp(x, y_sb, y_ssd, y_mem, proj, w_sb, w_ssd, w_mem, w_o, g_mix,
               g_pre, w_up, w_down, g_post, tm):
    m, d = x.shape
    inner = y_ssd.shape[1]
    dc = d // LANES
    gate0 = proj.shape[1] // dc - N_GATES
    tile = lambda w: pl.BlockSpec((tm, w), lambda i: (i, 0))
    gate = lambda k: pl.BlockSpec((tm // LANES, dc, LANES, LANES), lambda i: (i, gate0 + k, 0, 0))
    full = lambda a: pl.BlockSpec(a.shape, lambda i: (0, 0), pipeline_mode=pl.Buffered(1))
    return pl.pallas_call(
        functools.partial(_merge_mlp_kernel, chunk=1024), grid=(m // tm,),
        in_specs=[tile(d), tile(d), tile(inner), tile(d), gate(0), gate(1), gate(2),
                  full(w_sb), full(w_ssd), full(w_mem), full(w_o), full(g_mix),
                  full(g_pre), full(w_up), full(w_down), full(g_post)],
        out_specs=tile(d),
        out_shape=jax.ShapeDtypeStruct((m, d), F32),
        compiler_params=pltpu.CompilerParams(
            dimension_semantics=("arbitrary",), vmem_limit_bytes=VMEM_LIMIT),
        name="merge_mlp",
    )(x, y_sb, y_ssd, y_mem, proj, proj, proj, w_sb, w_ssd, w_mem, w_o, g_mix,
      g_pre, w_up, w_down, g_post)


def _layer(h, mem, norm_mix_pre, w_in, conv_w, conv_b, dt_bias, a_log, d_skip, ssd_norm,
           norm_mem, w_mem_kv, w_sb_out, w_ssd_out, w_mem_out, w_o, norm_mix_post,
           norm_mlp_pre, w_up, w_down, norm_mlp_post):
    b, s, d = h.shape
    m = b * s
    row = lambda v: v.reshape(1, -1)
    x2 = h.reshape(m, d)

    sb_w = 3 * SB_HEADS * SB_HEAD_DIM
    inner = SSD_HEADS * SSD_HEAD_DIM
    conv_dim = inner + 2 * SSD_GROUPS * SSD_STATE
    o_z, o_xbc, o_dt = sb_w, sb_w + inner, sb_w + inner + conv_dim
    o_memq = o_dt + SSD_HEADS
    o_gate = o_memq + MEM_HEADS * MEM_HEAD_DIM

    tn = 1024
    w_t = jnp.swapaxes(w_in, 0, 1).astype(BF16)
    w_dt_t = jnp.pad(w_t[o_dt:o_memq], ((0, LANES - SSD_HEADS), (0, 0)))
    a_tiles = list(range(sb_w // tn)) + list(range(o_dt // tn, (o_dt + o_gate - o_memq) // tn))
    z_tiles = list(range(o_z // tn, o_xbc // tn))
    qkv, proj, dt, dtt = _in_proj(x2, row(norm_mix_pre), w_t, (o_dt, o_memq), a_tiles, z_tiles, w_dt_t,
                                  SSD_HEADS, 2048, tn)
    qkv = qkv.reshape(b, s, -1)

    y_sb = _sb_attention(qkv, b, s, 1024)
    y_ssd = _ssd_branch(proj, dt, dtt, b, s, conv_w, conv_b, dt_bias, a_log, d_skip, ssd_norm)

    mlen = mem.shape[1]
    kv = _norm_proj(mem.reshape(b * mlen, d), row(norm_mem), w_mem_kv.astype(BF16), BF16, b * mlen, 1024)
    y_mem = _mem_attention(qkv, kv.reshape(b, mlen, -1), b, s, 1024)

    out = _merge_mlp(x2, y_sb.reshape(m, -1), y_ssd.reshape(m, -1), y_mem.reshape(m, -1), proj,
                     w_sb_out.astype(BF16), w_ssd_out.astype(BF16), w_mem_out.astype(BF16),
                     w_o.astype(BF16), row(norm_mix_post),
                     row(norm_mlp_pre), w_up.astype(BF16), w_down.astype(BF16), row(norm_mlp_post), 256)
    return out.reshape(b, s, d)


def kernel(x, mem, norm_mix_pre, w_in, conv_w, conv_b, dt_bias, a_log, d_skip, ssd_norm, norm_mem, w_mem_kv, w_sb_out, w_ssd_out, w_mem_out, w_o, norm_mix_post, norm_mlp_pre, w_up, w_down, norm_mlp_post):
    h = x
    for layer in range(w_in.shape[0]):
        h = _layer(h, mem, norm_mix_pre[layer], w_in[layer], conv_w[layer], conv_b[layer],
                   dt_bias[layer], a_log[layer], d_skip[layer], ssd_norm[layer], norm_mem[layer],
                   w_mem_kv[layer], w_sb_out[layer], w_ssd_out[layer], w_mem_out[layer], w_o[layer],
                   norm_mix_post[layer], norm_mlp_pre[layer], w_up[layer], w_down[layer],
                   norm_mlp_post[layer])
    return h
```

```python
import functools

import jax
import jax.numpy as jnp
import numpy as np
from jax import lax
from jax.experimental import pallas as pl
from jax.experimental.pallas import tpu as pltpu

F32 = jnp.float32
BF16 = jnp.bfloat16

EPS = 1e-6
LOG2E = 1.4426950408889634
LANES = 128
SB_HEADS = 16
SB_HEAD_DIM = 64
SB_BLOCK = 128
SSD_HEADS = 32
SSD_HEAD_DIM = 64
SSD_GROUPS = 4
SSD_STATE = 128
SSD_CHUNK = 128
SSD_CONV = 4
MEM_HEADS = 4
MEM_HEAD_DIM = 256
N_GATES = 3
VMEM_LIMIT = 56 * 1024 * 1024

SB_EXP_UNDERFLOW = 110.0
SB_NO_KEYS = 1e30


SSD_STRIDE = 4
SSD_CHUNKS_PER_STEP = 4


def _ssd_row_time(pos):
    span = 8 * SSD_STRIDE
    return (pos & -span) + (pos & 7) * SSD_STRIDE + ((pos >> 3) & (SSD_STRIDE - 1))


def _dot(a, b):
    return jnp.dot(a, b, preferred_element_type=F32)


def _dot_nt(a, b):
    return lax.dot_general(a, b, (((1,), (1,)), ((), ())), preferred_element_type=F32)


def _split_bf16(x, parts):
    out = []
    rem = x
    for _ in range(parts):
        p = rem.astype(BF16)
        out.append(p)
        rem = rem - p.astype(F32)
    return out


def _dot_split_lhs(x, m, parts):
    acc = None
    for p in _split_bf16(x, parts):
        t = _dot(p, m)
        acc = t if acc is None else acc + t
    return acc


def _dot_split_rhs(m, x, parts):
    acc = None
    for p in _split_bf16(x, parts):
        t = _dot(m, p)
        acc = t if acc is None else acc + t
    return acc


def _rms_rows(x, gain):
    ms = jnp.mean(x * x, axis=-1, keepdims=True)
    return x * lax.rsqrt(ms + EPS) * gain


def _softplus(x):
    return jnp.maximum(x, 0.0) + jnp.log1p(jnp.exp(-jnp.abs(x)))


def _sigmoid(x):
    return 1.0 / (1.0 + jnp.exp2(x * -LOG2E))


def _silu(x):
    h = 0.5 * x
    return h + h * jnp.tanh(h)


def _norm_proj_kernel(x_ref, g_ref, w_ref, o_ref, u_ref):
    @pl.when(pl.program_id(1) == 0)
    def _():
        u_ref[...] = _rms_rows(x_ref[...], g_ref[...]).astype(BF16)

    o_ref[...] = _dot(u_ref[...], w_ref[...]).astype(o_ref.dtype)


def _norm_proj(x, gain, w, out_dtype, tm, tn):
    m, d = x.shape
    n = w.shape[1]
    return pl.pallas_call(
        _norm_proj_kernel, grid=(m // tm, n // tn),
        in_specs=[pl.BlockSpec((tm, d), lambda i, j: (i, 0)),
                  pl.BlockSpec((1, d), lambda i, j: (0, 0)),
                  pl.BlockSpec((d, tn), lambda i, j: (0, j))],
        out_specs=pl.BlockSpec((tm, tn), lambda i, j: (i, j)),
        out_shape=jax.ShapeDtypeStruct((m, n), out_dtype),
        scratch_shapes=[pltpu.VMEM((tm, d), BF16)],
        compiler_params=pltpu.CompilerParams(dimension_semantics=("arbitrary", "arbitrary"),
                                             vmem_limit_bytes=VMEM_LIMIT),
        name="norm_proj",
    )(x, gain, w)


def _in_proj_kernel(x_ref, g_ref, wt_ref, wdtt_ref, oa_ref, oc_ref, dt_ref, dtt_ref, u_ref, *,
                    nt, a_tiles, silu_tiles, nh):
    j = pl.program_id(1)

    @pl.when(j == 0)
    def _():
        u = _rms_rows(x_ref[...], g_ref[...]).astype(BF16)
        u_ref[...] = u
        dt_ref[...] = _dot_nt(u, wdtt_ref[...])
        dtt_ref[...] = _dot_nt(wdtt_ref[0:nh, :], u)

    tile_in = lambda tiles: functools.reduce(jnp.logical_or, [j == t for t in tiles])
    plain_tiles = [t for t in range(nt) if t not in a_tiles and t not in silu_tiles]

    def store_chunked(res):
        for r in range(oc_ref.shape[0]):
            for c in range(oc_ref.shape[1]):
                oc_ref[r, c] = res[r * LANES:(r + 1) * LANES, c * LANES:(c + 1) * LANES]

    @pl.when(tile_in(a_tiles))
    def _():
        oa_ref[...] = _dot_nt(u_ref[...], wt_ref[...]).astype(oa_ref.dtype)

    @pl.when(tile_in(silu_tiles))
    def _():
        store_chunked(_silu(_dot_nt(u_ref[...], wt_ref[...])))

    @pl.when(tile_in(plain_tiles))
    def _():
        store_chunked(_dot_nt(u_ref[...], wt_ref[...]))


def _in_proj(x, gain, w_t, skip, a_tiles, silu_tiles, w_dt_t, nh, tm, tn):
    m, d = x.shape
    nt = (w_t.shape[0] - (skip[1] - skip[0])) // tn
    c_tiles = [t for t in range(nt) if t not in a_tiles]
    skip_tile, skip_rows = skip[0] // tn, skip[1] - skip[0]

    def rank(tiles, j):
        return jnp.maximum(sum((j >= t).astype(jnp.int32) for t in tiles) - 1, 0)

    return pl.pallas_call(
        functools.partial(_in_proj_kernel, nt=nt, a_tiles=tuple(a_tiles), silu_tiles=tuple(silu_tiles), nh=nh),
        grid=(m // tm, nt),
        in_specs=[pl.BlockSpec((tm, d), lambda i, j: (i, 0), pipeline_mode=pl.Buffered(1)),
                  pl.BlockSpec((1, d), lambda i, j: (0, 0)),
                  pl.BlockSpec((pl.Element(tn), pl.Element(d)),
                               lambda i, j: ((j * (tn // skip_rows) + (j >= skip_tile).astype(jnp.int32)) * skip_rows, 0)),
                  pl.BlockSpec((LANES, d), lambda i, j: (0, 0))],
        out_specs=[pl.BlockSpec((tm, tn), lambda i, j: (i, rank(a_tiles, j))),
                   pl.BlockSpec((tm // LANES, tn // LANES, LANES, LANES), lambda i, j: (i, rank(c_tiles, j), 0, 0)),
                   pl.BlockSpec((tm, LANES), lambda i, j: (i, 0)),
                   pl.BlockSpec((nh, tm), lambda i, j: (0, i))],
        out_shape=[jax.ShapeDtypeStruct((m, len(a_tiles) * tn), BF16),
                   jax.ShapeDtypeStruct((m // LANES, len(c_tiles) * tn // LANES, LANES, LANES), F32),
                   jax.ShapeDtypeStruct((m, LANES), F32),
                   jax.ShapeDtypeStruct((nh, m), F32)],
        scratch_shapes=[pltpu.VMEM((tm, d), BF16)],
        compiler_params=pltpu.CompilerParams(dimension_semantics=("arbitrary", "arbitrary"),
                                             vmem_limit_bytes=VMEM_LIMIT),
        name="in_proj",
    )(x, gain, w_t, w_dt_t)


def _sb_kernel(q_ref, k_ref, v_ref, mm_ref, o_ref, qs_ref, kc_ref, vc_ref, acc_ref, c_ref, *, nsub):
    qi = pl.program_id(2)
    blk = SB_BLOCK
    nblk = v_ref.shape[1] // blk
    head0 = lax.broadcasted_iota(jnp.int32, (blk, LANES), 1) < SB_HEAD_DIM
    key = jnp.bitwise_and(lax.broadcasted_iota(jnp.int32, (blk, 2 * blk), 1), blk - 1)
    causal = key < lax.broadcasted_iota(jnp.int32, (blk, 2 * blk), 0)
    scale = SB_HEAD_DIM ** -0.5

    keep0 = jnp.where(head0, 1.0, 0.0).astype(BF16)
    keep1 = jnp.where(head0, 0.0, 1.0).astype(BF16)

    def stack_heads(x):
        return jnp.concatenate([x * keep0, x * keep1], axis=0)

    @pl.when(qi == 0)
    def _():
        def fill(j, carry):
            off = pl.multiple_of(j * blk, blk)
            kc_ref[j] = stack_heads(k_ref[0, pl.ds(off, blk), :])
            vc_ref[j] = stack_heads(v_ref[0, pl.ds(off, blk), :])
            return carry
        lax.fori_loop(0, nblk, fill, 0)

    qs_ref[...] = (q_ref[0].astype(F32) * scale).astype(BF16)

    def visit(s, spans, first):
        units = [(t, i) for t in range(nsub) for i in range(len(spans))]
        dist = {u: spans[u[1]][0] for u in units}
        rows = {u: slice(spans[u[1]][1], spans[u[1]][2]) for u in units}
        diag = {u: first and dist[u] == 0 for u in units}
        js = {u: qi * nsub + u[0] - (s + dist[u]) for u in units}
        jcs = {u: js[u] if diag[u] else jnp.maximum(js[u], 0) for u in units}
        zs = {u: _dot_nt(qs_ref[u[0] * blk + rows[u].start:u[0] * blk + rows[u].stop, :], kc_ref[jcs[u]])
              for u in units}
        log_betas, sums = {}, {}
        for u in units:
            z = zs[u]
            sp = jnp.log(1.0 + jnp.exp2(jnp.abs(z) * -LOG2E))
            log_beta = jnp.minimum(z, 0.0) - sp
            log_keep = log_beta - z
            if diag[u]:
                log_keep = jnp.where(causal, log_keep, 0.0)
            log_betas[u] = log_beta
            hi, lo = _split_bf16(log_keep, 2)
            sums[u] = [_dot(jnp.concatenate([hi[:, h * blk:(h + 1) * blk], lo[:, h * blk:(h + 1) * blk]], axis=1),
                            mm_ref[...]) for h in range(2)]

        def put(full, part, r):
            pieces = ([full[:r.start]] if r.start > 0 else []) + [part]
            pieces += [full[r.stop:]] if r.stop < full.shape[0] else []
            return pieces[0] if len(pieces) == 1 else jnp.concatenate(pieces, axis=0)

        cmax = None
        for t in range(nsub):
            c = None if first else c_ref[t]
            acc = None if first else acc_ref[t]
            for i in range(len(spans)):
                u = (t, i)
                r0, r1 = sums[u]
                later = jnp.concatenate([r0[:, :blk], r1[:, :blk]], axis=1)
                total = jnp.concatenate([r0[:, blk:], r1[:, blk:]], axis=1)
                if diag[u]:
                    w = jnp.where(causal, jnp.exp(log_betas[u] + later), 0.0)
                    c = total
                    acc = _dot(w.astype(BF16), vc_ref[jcs[u]])
                else:
                    c_rows = jnp.where(js[u] >= 0, c[rows[u]], -SB_NO_KEYS)
                    w = jnp.exp(log_betas[u] + later + c_rows)
                    c = put(c, c_rows + total, rows[u])
                    acc = put(acc, acc[rows[u]] + _dot(w.astype(BF16), vc_ref[jcs[u]]), rows[u])
            acc_ref[t] = acc
            c_ref[t] = c
            cmax = c if cmax is None else jnp.maximum(cmax, c)
        return jnp.max(cmax)

    half = blk // 2
    last = qi * nsub + nsub - 1
    alive = lambda cm: cm > -SB_EXP_UNDERFLOW

    cm = visit(0, [(0, 0, blk), (1, 0, blk), (2, 0, half)], True)
    cm_low = jnp.max(functools.reduce(jnp.maximum, [c_ref[t, half:, :] for t in range(nsub)]))
    cm = lax.cond(alive(cm_low), lambda: visit(2, [(0, half, blk)], False), lambda: cm)

    def cond(st):
        return jnp.logical_and(st[0] <= last, alive(st[1]))

    def body(st):
        return st[0] + 1, visit(st[0], [(0, 0, blk)], False)

    lax.while_loop(cond, body, (jnp.int32(3), cm))
    for t in range(nsub):
        o_ref[0, t * blk:(t + 1) * blk, :] = acc_ref[t].astype(o_ref.dtype)


def _sb_attention(qkv, b, s, tq):
    pairs = SB_HEADS * SB_HEAD_DIM // LANES
    nsub = tq // SB_BLOCK
    idx = np.arange(SB_BLOCK)
    later = (idx[:, None] > idx[None, :]).astype(np.float32)
    m = np.concatenate([later, np.ones((SB_BLOCK, SB_BLOCK), np.float32)], axis=1)
    mm = jnp.asarray(np.concatenate([m, m], axis=0), BF16)
    return pl.pallas_call(
        functools.partial(_sb_kernel, nsub=nsub),
        grid=(b, pairs, s // tq),
        in_specs=[
            pl.BlockSpec((1, tq, LANES), lambda bi, hp, qi: (bi, qi, hp)),
            pl.BlockSpec((1, s, LANES), lambda bi, hp, qi: (bi, 0, pairs + hp)),
            pl.BlockSpec((1, s, LANES), lambda bi, hp, qi: (bi, 0, 2 * pairs + hp)),
            pl.BlockSpec((2 * SB_BLOCK, 2 * SB_BLOCK), lambda bi, hp, qi: (0, 0)),
        ],
        out_specs=pl.BlockSpec((1, tq, LANES), lambda bi, hp, qi: (bi, qi, hp)),
        out_shape=jax.ShapeDtypeStruct((b, s, SB_HEADS * SB_HEAD_DIM), BF16),
        scratch_shapes=[pltpu.VMEM((tq, LANES), BF16),
                        pltpu.VMEM((s // SB_BLOCK, 2 * SB_BLOCK, LANES), BF16),
                        pltpu.VMEM((s // SB_BLOCK, 2 * SB_BLOCK, LANES), BF16),
                        pltpu.VMEM((nsub, SB_BLOCK, LANES), F32),
                        pltpu.VMEM((nsub, SB_BLOCK, 2 * SB_BLOCK), F32)],
        compiler_params=pltpu.CompilerParams(
            dimension_semantics=("arbitrary", "arbitrary", "arbitrary"), vmem_limit_bytes=VMEM_LIMIT),
        name="sb_attention",
    )(qkv, qkv, qkv, mm)


def _ssd_kernel(z_ref, xs_ref, bc_ref, dt_ref, dtt_ref, cw_ref, cb_ref,
                dtb_ref, dtbt_ref, alog_ref, alogt_ref, dexp_ref, gn_ref,
                tri_ref, trit_ref, e_ref,
                o_ref, prev_ref, state_ref, yout_ref):
    ck = SSD_CHUNK
    stride = SSD_STRIDE
    span = 8 * stride
    inner = SSD_HEADS * SSD_HEAD_DIM
    gw = inner // SSD_GROUPS
    gs = SSD_GROUPS * SSD_STATE
    nx = xs_ref.shape[1]
    tile_starts = [g * span + i for g in range(ck // span) for i in range(stride)]
    tile_rows = [pl.ds(start, 8, stride=stride) for start in tile_starts]
    ntile = len(tile_rows)

    @pl.when(pl.program_id(1) == 0)
    def _():
        prev_ref[...] = jnp.zeros_like(prev_ref)
        state_ref[...] = jnp.zeros_like(state_ref)

    first_row = lax.broadcasted_iota(jnp.int32, (8, LANES), 0) == 0
    lane = lax.broadcasted_iota(jnp.int32, (ck, LANES), 1)
    row = lax.broadcasted_iota(jnp.int32, (ck, LANES), 0)
    head0 = lane < SSD_HEAD_DIM
    tri_mask = _ssd_row_time(lane) <= _ssd_row_time(row)
    heads_per_group = SSD_HEADS // SSD_GROUPS

    def conv_silu(ch, c):
        lanes = slice(c * LANES, (c + 1) * LANES)
        src_ref, sc = (xs_ref, c) if c < nx else (bc_ref, c - nx)
        tiles = [src_ref[ch, sc, rows, :] for rows in tile_rows]

        def before(j):
            return prev_ref[j - 1, :, lanes] if ch == 0 else src_ref[ch - 1, sc, tile_rows[ntile - j], :]

        down = {-j: pltpu.roll(before(j), 1, 0) for j in range(1, SSD_CONV)}
        for n, tile in enumerate(tiles):
            if n % stride >= stride - (SSD_CONV - 1):
                down[n] = pltpu.roll(tile, 1, 0)
        wrapped = {}

        def back(n, k):
            if n % stride >= k:
                return tiles[n - k]
            if n - k not in wrapped:
                wrapped[n - k] = jnp.where(first_row, down[n - k], down[n - k + stride])
            return wrapped[n - k]

        out = []
        for n in range(ntile):
            acc = cb_ref[:, lanes]
            for k in range(SSD_CONV):
                acc = acc + back(n, k) * cw_ref[SSD_CONV - 1 - k:SSD_CONV - k, lanes]
            out.append(_silu(acc))
        if ch == z_ref.shape[0] - 1:
            for j in range(1, SSD_CONV):
                prev_ref[j - 1, :, lanes] = tiles[ntile - j]
        return jnp.concatenate(out, axis=0)

    for ch in range(z_ref.shape[0]):
        rows_ch = slice(ch * ck, (ch + 1) * ck)
        dt_raw = jnp.concatenate([dt_ref[pl.ds(ch * ck + start, 8, stride=stride), :] for start in tile_starts], axis=0)
        dt = _softplus(dt_raw + dtb_ref[...])
        dtt = _softplus(dtt_ref[:, rows_ch] + dtbt_ref[...])
        da = dt * (-jnp.exp(alog_ref[...]))
        dat = dtt * (-jnp.exp(alogt_ref[...]))
        a_cs = _dot_split_rhs(tri_ref[...], da, 3)
        a_cst = _dot_split_lhs(dat, trit_ref[...], 3)
        a_last = a_cs[ck - 1:ck, :]

        ea = jnp.exp(a_cs)
        dte = jnp.exp(a_last - a_cs)

        for g in range(SSD_GROUPS):
            cols = slice(g * gw, (g + 1) * gw)
            chunks = range(g * gw // LANES, (g + 1) * gw // LANES)
            xs = jnp.concatenate([conv_silu(ch, c) for c in chunks], axis=1)
            bg = conv_silu(ch, nx + g)
            cg16 = conv_silu(ch, nx + SSD_GROUPS + g).astype(BF16)
            e = e_ref[:, cols]
            ea_x = _dot_split_lhs(ea, e, 2)
            x_dt = xs * _dot_split_lhs(dt, e, 2)
            x_dt16 = x_dt.astype(BF16)
            x_end16 = (x_dt * _dot_split_lhs(dte, e, 2)).astype(BF16)

            cb = _dot_nt(cg16, bg.astype(BF16))
            y_pairs = []
            for rp in range(heads_per_group // 2):
                xp = x_dt16[:, rp * LANES:(rp + 1) * LANES]
                ys = []
                for hh in range(2):
                    r = g * heads_per_group + 2 * rp + hh
                    seg = a_cs[:, r:r + 1] - a_cst[r:r + 1, :]
                    lmat = cb * jnp.exp(jnp.where(tri_mask, seg, -jnp.inf))
                    ys.append(_dot(lmat.astype(BF16), xp))
                y_pairs.append(jnp.where(head0, ys[0], ys[1]))

            state = state_ref[:, cols]
            y_off = _dot(cg16, state.astype(BF16)) * ea_x
            contrib = _dot(bg.T.astype(BF16), x_end16)
            state_ref[:, cols] = state * ea_x[ck - 1:ck, :] + contrib

            y = jnp.concatenate(y_pairs, axis=1) + y_off + dexp_ref[:, cols] * xs
            for c in chunks:
                for n, rows in enumerate(tile_rows):
                    yout_ref[ch, c, rows, :] = y[8 * n:8 * n + 8, (c - chunks[0]) * LANES:(c - chunks[0] + 1) * LANES]
            y = jnp.concatenate([yout_ref[ch, c] for c in chunks], axis=1)
            y = y * jnp.concatenate([z_ref[ch, c] for c in chunks], axis=1)
            ms = jnp.mean(y * y, axis=-1, keepdims=True)
            o_ref[0, rows_ch, cols] = (y * lax.rsqrt(ms + EPS) * gn_ref[:, cols]).astype(o_ref.dtype)


def _ssd_branch(proj, dt, dtt, b, s, conv_w, conv_b, dt_bias, a_log, d_skip, ssd_norm):
    inner = SSD_HEADS * SSD_HEAD_DIM
    gs = SSD_GROUPS * SSD_STATE
    per = SSD_CHUNKS_PER_STEP
    nc = s // (SSD_CHUNK * per)
    nx, nbc = inner // LANES, 2 * gs // LANES
    chunked = lambda n, at: pl.BlockSpec((per, n, SSD_CHUNK, LANES), lambda bi, ci: (bi * nc + ci, at // n, 0, 0))
    idx = np.arange(SSD_CHUNK)
    when = _ssd_row_time(idx)
    tri = jnp.asarray((when[None, :] <= when[:, None]).astype(np.float32), BF16)
    trit = jnp.asarray((idx[:, None] <= when[None, :]).astype(np.float32), BF16)
    expand = np.zeros((LANES, inner), np.float32)
    expand[np.arange(inner) // SSD_HEAD_DIM, np.arange(inner)] = 1.0
    expand = jnp.asarray(expand, BF16)

    pad = LANES - SSD_HEADS
    row = lambda v: v.reshape(1, -1)
    dtb = jnp.pad(row(dt_bias), ((0, 0), (0, pad)))
    alog = jnp.pad(row(a_log), ((0, 0), (0, pad)))
    dexp = row(jnp.repeat(d_skip, SSD_HEAD_DIM))

    const = lambda shape: pl.BlockSpec(shape, lambda bi, ci: (0,) * len(shape))
    return pl.pallas_call(
        _ssd_kernel, grid=(b, nc),
        in_specs=[
            chunked(nx, 0),
            chunked(nx, nx),
            chunked(nbc, 2 * nx),
            pl.BlockSpec((per * SSD_CHUNK, LANES), lambda bi, ci: (bi * nc + ci, 0)),
            pl.BlockSpec((SSD_HEADS, per * SSD_CHUNK), lambda bi, ci: (0, bi * nc + ci)),
            const((SSD_CONV, inner + 2 * gs)), const((1, inner + 2 * gs)),
            const((1, LANES)), const((SSD_HEADS, 1)), const((1, LANES)), const((SSD_HEADS, 1)),
            const((1, inner)), const((1, inner)),
            const((SSD_CHUNK, SSD_CHUNK)), const((SSD_CHUNK, SSD_CHUNK)), const((LANES, inner)),
        ],
        out_specs=pl.BlockSpec((1, per * SSD_CHUNK, inner), lambda bi, ci: (bi, ci, 0)),
        out_shape=jax.ShapeDtypeStruct((b, s, inner), BF16),
        scratch_shapes=[pltpu.VMEM((SSD_CONV - 1, 8, inner + 2 * gs), F32),
                        pltpu.VMEM((SSD_STATE, inner), F32),
                        pltpu.VMEM((per, inner // LANES, SSD_CHUNK, LANES), F32)],
        compiler_params=pltpu.CompilerParams(
            dimension_semantics=("arbitrary", "arbitrary"), vmem_limit_bytes=VMEM_LIMIT),
        name="ssd_branch",
    )(proj, proj, proj, dt, dtt, conv_w, row(conv_b),
      dtb, dt_bias.reshape(-1, 1), alog, a_log.reshape(-1, 1), dexp, row(ssd_norm),
      tri, trit, expand)


def _mem_attn_kernel(q_ref, kv_ref, o_ref):
    width = MEM_HEADS * MEM_HEAD_DIM
    scale = MEM_HEAD_DIM ** -0.5
    head_cols = [slice(h * MEM_HEAD_DIM, (h + 1) * MEM_HEAD_DIM) for h in range(MEM_HEADS)]
    scores = [_dot_nt(q_ref[0, :, cols], kv_ref[0, :, cols]) * scale for cols in head_cols]
    probs = []
    for sc in scores:
        p = jnp.exp(sc - jnp.max(sc, axis=-1, keepdims=True))
        probs.append((p / jnp.sum(p, axis=-1, keepdims=True)).astype(BF16))
    for h, cols in enumerate(head_cols):
        v = kv_ref[0, :, width + h * MEM_HEAD_DIM:width + (h + 1) * MEM_HEAD_DIM]
        o_ref[0, :, cols] = _dot(probs[h], v).astype(o_ref.dtype)


def _mem_attention(qkv, kv, b, s, tq):
    width = MEM_HEADS * MEM_HEAD_DIM
    mlen = kv.shape[1]
    return pl.pallas_call(
        _mem_attn_kernel, grid=(b, s // tq),
        in_specs=[pl.BlockSpec((1, tq, width), lambda bi, qi: (bi, qi, 3)),
                  pl.BlockSpec((1, mlen, 2 * width), lambda bi, qi: (bi, 0, 0))],
        out_specs=pl.BlockSpec((1, tq, width), lambda bi, qi: (bi, qi, 0)),
        out_shape=jax.ShapeDtypeStruct((b, s, width), BF16),
        compiler_params=pltpu.CompilerParams(
            dimension_semantics=("arbitrary", "arbitrary"), vmem_limit_bytes=VMEM_LIMIT),
        name="mem_attention",
    )(qkv, kv)


def _merge_mlp_kernel(x_ref, ysb_ref, yssd_ref, ymem_ref, g0_ref, g1_ref, g2_ref,
                      wsb_ref, wssd_ref, wmem_ref, wo_ref, gmix_ref,
                      gpre_ref, wup_ref, wdown_ref, gpost_ref, o_ref, *, chunk):
    def gate(g_ref):
        logits = jnp.concatenate([jnp.concatenate([g_ref[r, c] for c in range(g_ref.shape[1])], axis=1)
                                  for r in range(g_ref.shape[0])], axis=0)
        return _sigmoid(logits)

    merged = (gate(g0_ref) * _dot(ysb_ref[...], wsb_ref[...])
              + gate(g1_ref) * _dot(yssd_ref[...], wssd_ref[...])
              + gate(g2_ref) * _dot(ymem_ref[...], wmem_ref[...]))
    mix = _dot(merged.astype(BF16), wo_ref[...])
    h = x_ref[...] + _rms_rows(mix, gmix_ref[...])

    u = _rms_rows(h, gpre_ref[...]).astype(BF16)
    ff = None
    for c in range(wup_ref.shape[1] // chunk):
        hid = _dot(u, wup_ref[:, c * chunk:(c + 1) * chunk])
        act = jnp.square(jnp.maximum(hid, 0.0)).astype(BF16)
        t = _dot(act, wdown_ref[c * chunk:(c + 1) * chunk, :])
        ff = t if ff is None else ff + t
    o_ref[...] = h + _rms_rows(ff, gpost_ref[...])


def _merge_mlp(x, y_sb, y_ssd, y_mem, proj, w_sb, w_ssd, w_mem, w_o, g_mix,
               g_pre, w_up, w_down, g_post, tm):
    m, d = x.shape
    inner = y_ssd.shape[1]
    dc = d // LANES
    gate0 = proj.shape[1] // dc - N_GATES
    tile = lambda w: pl.BlockSpec((tm, w), lambda i: (i, 0))
    gate = lambda k: pl.BlockSpec((tm // LANES, dc, LANES, LANES), lambda i: (i, gate0 + k, 0, 0))
    full = lambda a: pl.BlockSpec(a.shape, lambda i: (0, 0), pipeline_mode=pl.Buffered(1))
    return pl.pallas_call(
        functools.partial(_merge_mlp_kernel, chunk=1024), grid=(m // tm,),
        in_specs=[tile(d), tile(d), tile(inner), tile(d), gate(0), gate(1), gate(2),
                  full(w_sb), full(w_ssd), full(w_mem), full(w_o), full(g_mix),
                  full(g_pre), full(w_up), full(w_down), full(g_post)],
        out_specs=tile(d),
        out_shape=jax.ShapeDtypeStruct((m, d), F32),
        compiler_params=pltpu.CompilerParams(
            dimension_semantics=("arbitrary",), vmem_limit_bytes=VMEM_LIMIT),
        name="merge_mlp",
    )(x, y_sb, y_ssd, y_mem, proj, proj, proj, w_sb, w_ssd, w_mem, w_o, g_mix,
      g_pre, w_up, w_down, g_post)


def _layer(h, mem, norm_mix_pre, w_in, conv_w, conv_b, dt_bias, a_log, d_skip, ssd_norm,
           norm_mem, w_mem_kv, w_sb_out, w_ssd_out, w_mem_out, w_o, norm_mix_post,
           norm_mlp_pre, w_up, w_down, norm_mlp_post):
    b, s, d = h.shape
    m = b * s
    row = lambda v: v.reshape(1, -1)
    x2 = h.reshape(m, d)

    sb_w = 3 * SB_HEADS * SB_HEAD_DIM
    inner = SSD_HEADS * SSD_HEAD_DIM
    conv_dim = inner + 2 * SSD_GROUPS * SSD_STATE
    o_z, o_xbc, o_dt = sb_w, sb_w + inner, sb_w + inner + conv_dim
    o_memq = o_dt + SSD_HEADS
    o_gate = o_memq + MEM_HEADS * MEM_HEAD_DIM

    tn = 1024
    w_t = jnp.swapaxes(w_in, 0, 1).astype(BF16)
    w_dt_t = jnp.pad(w_t[o_dt:o_memq], ((0, LANES - SSD_HEADS), (0, 0)))
    a_tiles = list(range(sb_w // tn)) + list(range(o_dt // tn, (o_dt + o_gate - o_memq) // tn))
    z_tiles = list(range(o_z // tn, o_xbc // tn))
    qkv, proj, dt, dtt = _in_proj(x2, row(norm_mix_pre), w_t, (o_dt, o_memq), a_tiles, z_tiles, w_dt_t,
                                  SSD_HEADS, 2048, tn)
    qkv = qkv.reshape(b, s, -1)

    y_sb = _sb_attention(qkv, b, s, 1024)
    y_ssd = _ssd_branch(proj, dt, dtt, b, s, conv_w, conv_b, dt_bias, a_log, d_skip, ssd_norm)

    mlen = mem.shape[1]
    kv = _norm_proj(mem.reshape(b * mlen, d), row(norm_mem), w_mem_kv.astype(BF16), BF16, b * mlen, 1024)
    y_mem = _mem_attention(qkv, kv.reshape(b, mlen, -1), b, s, 1024)

    out = _merge_mlp(x2, y_sb.reshape(m, -1), y_ssd.reshape(m, -1), y_mem.reshape(m, -1), proj,
                     w_sb_out.astype(BF16), w_ssd_out.astype(BF16), w_mem_out.astype(BF16),
                     w_o.astype(BF16), row(norm_mix_post),
                     row(norm_mlp_pre), w_up.astype(BF16), w_down.astype(BF16), row(norm_mlp_post), 256)
    return out.reshape(b, s, d)


def kernel(x, mem, norm_mix_pre, w_in, conv_w, conv_b, dt_bias, a_log, d_skip, ssd_norm, norm_mem, w_mem_kv, w_sb_out, w_ssd_out, w_mem_out, w_o, norm_mix_post, norm_mlp_pre, w_up, w_down, norm_mlp_post):
    h = x
    for layer in range(w_in.shape[0]):
        h = _layer(h, mem, norm_mix_pre[layer], w_in[layer], conv_w[layer], conv_b[layer],
                   dt_bias[layer], a_log[layer], d_skip[layer], ssd_norm[layer], norm_mem[layer],
                   w_mem_kv[layer], w_sb_out[layer], w_ssd_out[layer], w_mem_out[layer], w_o[layer],
                   norm_mix_post[layer], norm_mlp_pre[layer], w_up[layer], w_down[layer],
                   norm_mlp_post[layer])
    return h
```
